```python
import math
import jax, jax.numpy as jnp
from jax import lax
import numpy as np

D_MODEL = 1024
BATCH = 8
SEQ = 2048
DEPTH = 4

D_MIX = D_MODEL
ATTN_WIDTH = D_MIX // 2
N_SB_HEADS = 8
SB_HEAD_DIM = ATTN_WIDTH // N_SB_HEADS
POOL_WINDOWS = (2, 4, 8, 16)
N_POOL_GROUPS = len(POOL_WINDOWS)
POOL_WIDTH = D_MIX // 4
POOL_GROUP_DIM = POOL_WIDTH // N_POOL_GROUPS
CONV_WIDTH = D_MIX - ATTN_WIDTH - POOL_WIDTH
CONV_KERNEL = 31
IN_COLS = 3 * ATTN_WIDTH + POOL_WIDTH + 2 * CONV_WIDTH
Q_BLOCK = 128

D_FF = 2816
N_EXPERTS = 8
TOP_K = 2
N_DENSE = (DEPTH + 1) // 2
N_MOE = DEPTH // 2

RMS_EPS = 1e-6
LN_EPS = 1e-5

kernel_name = "hybrid_sbattn_pool_conformer_moe"


def rmsnorm(x, g):
    x32 = x.astype(jnp.float32)
    y = x32 * lax.rsqrt(jnp.mean(x32 * x32, axis=-1, keepdims=True) + RMS_EPS)
    return (y * g.astype(jnp.float32)).astype(x.dtype)


def layernorm(x, g, b):
    x32 = x.astype(jnp.float32)
    mu = jnp.mean(x32, axis=-1, keepdims=True)
    xc = x32 - mu
    var = jnp.mean(xc * xc, axis=-1, keepdims=True)
    y = xc * lax.rsqrt(var + LN_EPS)
    return (y * g.astype(jnp.float32) + b.astype(jnp.float32)).astype(x.dtype)


def _sb_block(qb, k_pre, v_pre, q_start):
    scale = 1.0 / math.sqrt(SB_HEAD_DIM)
    z = jnp.einsum('bqhd,bkhd->bhqk', qb, k_pre).astype(jnp.float32) * scale
    t_idx = q_start + jnp.arange(qb.shape[1])
    s_idx = jnp.arange(k_pre.shape[1])
    mask = s_idx[None, :] < t_idx[:, None]
    log_1m_beta = jnp.where(mask, jax.nn.log_sigmoid(-z), 0.0)
    rest = lax.cumsum(log_1m_beta, axis=3, reverse=True) - log_1m_beta
    w = jnp.where(mask, jnp.exp(jax.nn.log_sigmoid(z) + rest), 0.0)
    return jnp.einsum('bhqk,bkhd->bqhd', w.astype(v_pre.dtype), v_pre)


def stick_breaking_attention(q, k, v):
    S = q.shape[1]
    outs = []
    for start in range(0, S, Q_BLOCK):
        end = start + Q_BLOCK
        outs.append(_sb_block(q[:, start:end], k[:, :end], v[:, :end], start))
    return jnp.concatenate(outs, axis=1)


def multiscale_pool(u, w_pool, pool_scale):
    B, S, _ = u.shape
    ug = u.reshape(B, S, N_POOL_GROUPS, POOL_GROUP_DIM).astype(jnp.float32)
    csum = jnp.cumsum(ug, axis=1)
    pos = jnp.arange(S)
    outs = []
    for g, win in enumerate(POOL_WINDOWS):
        cg = csum[:, :, g]
        shifted = jnp.pad(cg, ((0, 0), (win, 0), (0, 0)))[:, :S]
        count = jnp.minimum(pos + 1, win).astype(jnp.float32)
        outs.append((cg - shifted) / count[None, :, None] - ug[:, :, g])
    y = jnp.stack(outs, axis=2).astype(u.dtype)
    y = jnp.einsum('bsgc,gcd->bsgd', y, w_pool).reshape(B, S, POOL_WIDTH)
    return y * pool_scale


def conformer_conv(val, gate, conv_w, conv_b, ln_g, ln_b):
    u = val * jax.nn.sigmoid(gate)
    y = lax.conv_general_dilated(
        u, conv_w[:, None, :].astype(u.dtype), window_strides=(1,),
        padding=[(CONV_KERNEL - 1, 0)],
        dimension_numbers=('NWC', 'WIO', 'NWC'),
        feature_group_count=CONV_WIDTH) + conv_b
    y = layernorm(y, ln_g, ln_b)
    return jax.nn.silu(y)


def hybrid_mixer(h, w_in, pool_w, pool_scale, conv_w, conv_b, ln_g, ln_b, gn_g, w_out):
    B, S, _ = h.shape
    p = h @ w_in
    a0 = ATTN_WIDTH
    q, k, v, u_pool, c_val, c_gate = jnp.split(
        p, [a0, 2 * a0, 3 * a0, 3 * a0 + POOL_WIDTH, 3 * a0 + POOL_WIDTH + CONV_WIDTH], axis=-1)
    shp = (B, S, N_SB_HEADS, SB_HEAD_DIM)
    o_a = stick_breaking_attention(q.reshape(shp), k.reshape(shp), v.reshape(shp)).reshape(B, S, ATTN_WIDTH)
    o_b = multiscale_pool(u_pool, pool_w, pool_scale)
    o_c = conformer_conv(c_val, c_gate, conv_w, conv_b, ln_g, ln_b)
    g_a, g_b, g_c = jnp.split(gn_g, [ATTN_WIDTH, ATTN_WIDTH + POOL_WIDTH])
    o = jnp.concatenate([rmsnorm(o_a, g_a), rmsnorm(o_b, g_b), rmsnorm(o_c, g_c)], axis=-1)
    return o @ w_out


def swiglu(h, w_gate, w_up, w_down):
    return (jax.nn.silu(h @ w_gate) * (h @ w_up)) @ w_down


def moe_swiglu(h, w_router, w_gate, w_up, w_down):
    B, S, D = h.shape
    t = h.reshape(B * S, D)
    logits = (t @ w_router).astype(jnp.float32)
    top_v, top_i = lax.top_k(logits, TOP_K)
    gates = jax.nn.softmax(top_v, axis=-1)
    combine = jnp.sum(jax.nn.one_hot(top_i, N_EXPERTS, dtype=jnp.float32) * gates[..., None], axis=1)
    combine = combine.astype(h.dtype)
    out = jnp.zeros_like(t)
    for e in range(N_EXPERTS):
        out = out + combine[:, e:e + 1] * swiglu(t, w_gate[e], w_up[e], w_down[e])
    return out.reshape(B, S, D)


def setup_inputs(seed: int = 0) -> dict:
    key = jax.random.key(seed)
    ks = jax.random.split(key, 24)
    f32 = jnp.float32

    def nrm(k, shape, scale):
        return jax.random.normal(k, shape, f32) * scale

    def gain(k, shape):
        return 1.0 + 0.02 * jax.random.normal(k, shape, f32)

    return {
        "x": nrm(ks[0], (BATCH, SEQ, D_MODEL), 1.0),
        "attn_norm_g": gain(ks[1], (DEPTH, D_MODEL)),
        "w_in": nrm(ks[2], (DEPTH, D_MODEL, IN_COLS), D_MODEL ** -0.5),
        "pool_w": nrm(ks[3], (DEPTH, N_POOL_GROUPS, POOL_GROUP_DIM, POOL_GROUP_DIM), POOL_GROUP_DIM ** -0.5),
        "pool_scale": gain(ks[4], (DEPTH, POOL_WIDTH)),
        "conv_w": nrm(ks[5], (DEPTH, CONV_KERNEL, CONV_WIDTH), CONV_KERNEL ** -0.5),
        "conv_b": nrm(ks[6], (DEPTH, CONV_WIDTH), 0.02),
        "conv_ln_g": gain(ks[7], (DEPTH, CONV_WIDTH)),
        "conv_ln_b": nrm(ks[8], (DEPTH, CONV_WIDTH), 0.02),
        "group_norm_g": gain(ks[9], (DEPTH, D_MIX)),
        "w_out": nrm(ks[10], (DEPTH, D_MIX, D_MODEL), D_MIX ** -0.5),
        "ffn_norm_g": gain(ks[11], (DEPTH, D_MODEL)),
        "dense_w_gate": nrm(ks[12], (N_DENSE, D_MODEL, D_FF), D_MODEL ** -0.5),
        "dense_w_up": nrm(ks[13], (N_DENSE, D_MODEL, D_FF), D_MODEL ** -0.5),
        "dense_w_down": nrm(ks[14], (N_DENSE, D_FF, D_MODEL), D_FF ** -0.5),
        "router_w": nrm(ks[15], (N_MOE, D_MODEL, N_EXPERTS), D_MODEL ** -0.5),
        "moe_w_gate": nrm(ks[16], (N_MOE, N_EXPERTS, D_MODEL, D_FF), D_MODEL ** -0.5),
        "moe_w_up": nrm(ks[17], (N_MOE, N_EXPERTS, D_MODEL, D_FF), D_MODEL ** -0.5),
        "moe_w_down": nrm(ks[18], (N_MOE, N_EXPERTS, D_FF, D_MODEL), D_FF ** -0.5),
        "final_norm_g": gain(ks[19], (D_MODEL,)),
    }


def reference(x, attn_norm_g, w_in, pool_w, pool_scale, conv_w, conv_b, conv_ln_g, conv_ln_b,
              group_norm_g, w_out, ffn_norm_g, dense_w_gate, dense_w_up, dense_w_down,
              router_w, moe_w_gate, moe_w_up, moe_w_down, final_norm_g):
    for l in range(DEPTH):
        h = rmsnorm(x, attn_norm_g[l])
        x = x + hybrid_mixer(h, w_in[l], pool_w[l], pool_scale[l], conv_w[l], conv_b[l],
                             conv_ln_g[l], conv_ln_b[l], group_norm_g[l], w_out[l])
        h = rmsnorm(x, ffn_norm_g[l])
        if l % 2 == 0:
            i = l // 2
            x = x + swiglu(h, dense_w_gate[i], dense_w_up[i], dense_w_down[i])
        else:
            i = l // 2
            x = x + moe_swiglu(h, router_w[i], moe_w_gate[i], moe_w_up[i], moe_w_down[i])
    return rmsnorm(x, final_norm_g)
```

```python
import functools
import math

import jax
import jax.numpy as jnp
from jax import lax
from jax.experimental import pallas as pl
from jax.experimental.pallas import tpu as pltpu

F32 = jnp.float32
BF16 = jnp.bfloat16

LANES = 128
SUBLANES = 8

N_HEADS = 8
HEAD_DIM = 64
ATTN_W = N_HEADS * HEAD_DIM
POOL_W = 256
CONV_W = 256
POOL_WINDOWS = (2, 4, 8, 16)
CONV_K = 31
HALO = 32
N_EXPERTS = 8
TOP_K = 2
RMS_EPS = 1e-6
LN_EPS = 1e-5

ATTN_BLOCK = 256
EXP_ZERO_BELOW = -105.0

TOKEN_TILE = 512
ROW_CHUNK = 128
GROUP_TILE = 256
FF_CHUNKS = (1024, 1024, 768)

VMEM_LIMIT = 56 * 1024 * 1024


def _cparams(n_axes, vmem=VMEM_LIMIT):
    return pltpu.CompilerParams(dimension_semantics=("arbitrary",) * n_axes, vmem_limit_bytes=vmem)


def _rms(x, g):
    return x * lax.rsqrt(jnp.mean(x * x, axis=-1, keepdims=True) + RMS_EPS) * g


def _dot(a, b):
    return jnp.dot(a, b, preferred_element_type=F32)


def _const_spec(shape):
    zeros = (0,) * len(shape)
    return pl.BlockSpec(shape, lambda *_: zeros, pipeline_mode=pl.Buffered(1))


def _inproj_kernel(x_ref, g_ref, w_ref, qkv_ref, rest_ref):
    h = _rms(x_ref[...], g_ref[...]).astype(BF16)
    nq = qkv_ref.shape[-1]
    qkv_ref[...] = _dot(h, w_ref[:, :nq]).astype(BF16)
    rest_ref[...] = _dot(h, w_ref[:, nq:])


def _inproj(x2, g, w_bf16):
    t, d = x2.shape
    n = w_bf16.shape[1]
    nq = 3 * ATTN_W
    tm = min(TOKEN_TILE, t)
    return pl.pallas_call(
        _inproj_kernel,
        grid=(t // tm,),
        in_specs=[pl.BlockSpec((tm, d), lambda i: (i, 0)),
                  _const_spec((1, d)),
                  _const_spec((d, n))],
        out_specs=[pl.BlockSpec((tm, nq), lambda i: (i, 0)),
                   pl.BlockSpec((tm, n - nq), lambda i: (i, 0))],
        out_shape=[jax.ShapeDtypeStruct((t, nq), BF16),
                   jax.ShapeDtypeStruct((t, n - nq), F32)],
        compiler_params=_cparams(1),
        name="inproj",
    )(x2, g.reshape(1, d), w_bf16)


def _attn_kernel(q_ref, k_ref, v_ref, o_ref):
    blk = q_ref.shape[1]
    qi = pl.program_id(2)
    scale = 1.0 / math.sqrt(HEAD_DIM)

    q = q_ref[0]
    lane = lax.broadcasted_iota(jnp.int32, (1, LANES), 1)
    head_lanes = (lane < HEAD_DIM, lane >= HEAD_DIM)
    zero_bf = jnp.zeros((), BF16)
    qh = tuple(jnp.where(m, q, zero_bf) for m in head_lanes)

    jj = lax.broadcasted_iota(jnp.int32, (2 * blk, blk), 0)
    ss = lax.broadcasted_iota(jnp.int32, (2 * blk, blk), 1)
    upper2 = jnp.where(jnp.where(jj >= blk, jj - blk, jj) > ss, 1.0, 0.0).astype(BF16)

    r_idx = lax.broadcasted_iota(jnp.int32, (blk, blk), 0)
    c_idx = lax.broadcasted_iota(jnp.int32, (blk, blk), 1)
    causal = c_idx < r_idx

    def one_block(j, carries, acc, diagonal):
        start = pl.multiple_of(j * blk, blk)
        k = k_ref[0, pl.ds(start, blk), :]
        v = v_ref[0, pl.ds(start, blk), :]
        new_carries = []
        for h in range(2):
            z = lax.dot_general(qh[h], k, (((1,), (1,)), ((), ())),
                                preferred_element_type=F32) * scale
            softplus = jnp.maximum(z, 0.0) + jnp.log(1.0 + jnp.exp(-jnp.abs(z)))
            log_1m_beta = -softplus
            if diagonal:
                log_1m_beta = jnp.where(causal, log_1m_beta, 0.0)
            hi = log_1m_beta.astype(BF16)
            lo = (log_1m_beta - hi.astype(F32)).astype(BF16)
            rest = _dot(jnp.concatenate([hi, lo], axis=1), upper2)
            w = jnp.exp((z - softplus) + rest + carries[h])
            if diagonal:
                w = jnp.where(causal, w, 0.0)
            vh = jnp.where(head_lanes[h], v, zero_bf)
            acc = acc + _dot(w.astype(BF16), vh)
            new_carries.append(carries[h] + rest[:, 0:1] + log_1m_beta[:, 0:1])
        return tuple(new_carries), acc

    def keep_going(carries):
        bound = jnp.max(jnp.maximum(carries[0], carries[1]))
        return (bound > EXP_ZERO_BELOW).astype(jnp.int32)

    zero_col = jnp.zeros((blk, 1), F32)
    carries, acc = one_block(qi, (zero_col, zero_col), jnp.zeros((blk, LANES), F32), True)

    def cond(state):
        return jnp.logical_and(state[0] >= 0, state[1] > 0)

    def body(state):
        j, _, c0, c1, acc = state
        (c0, c1), acc = one_block(j, (c0, c1), acc, False)
        return j - 1, keep_going((c0, c1)), c0, c1, acc

    state = lax.while_loop(cond, body, (qi - 1, keep_going(carries), carries[0], carries[1], acc))
    o_ref[0] = state[4]


def _attention(qkv3):
    b, s, _ = qkv3.shape
    blk = min(ATTN_BLOCK, s)
    n_pairs = ATTN_W // LANES
    return pl.pallas_call(
        _attn_kernel,
        grid=(b, n_pairs, s // blk),
        in_specs=[pl.BlockSpec((1, blk, LANES), lambda bi, hp, qi: (bi, qi, hp)),
                  pl.BlockSpec((1, s, LANES), lambda bi, hp, qi: (bi, 0, n_pairs + hp)),
                  pl.BlockSpec((1, s, LANES), lambda bi, hp, qi: (bi, 0, 2 * n_pairs + hp))],
        out_specs=pl.BlockSpec((1, blk, LANES), lambda bi, hp, qi: (bi, qi, hp)),
        out_shape=jax.ShapeDtypeStruct((b, s, ATTN_W), F32),
        compiler_params=_cparams(3),
        name="sb_attention",
    )(qkv3, qkv3, qkv3)


def _mixout_kernel(x_ref, rest_ref, halo_ref, oa_ref, wp_ref, ps_ref, cw_ref, cb_ref,
                   lg_ref, lb_ref, gn_ref, wo_ref, out_ref, pool_ext, conv_ext):
    tm = x_ref.shape[1]
    ti = pl.program_id(1)
    has_history = (ti > 0).astype(F32)

    def glu(val, gate):
        return val * (1.0 / (1.0 + jnp.exp(-gate)))

    halo = halo_ref[0] * has_history
    cur = rest_ref[0]
    pool_ext[0:HALO, :] = halo[:, 0:POOL_W]
    pool_ext[HALO:HALO + tm, :] = cur[:, 0:POOL_W]
    conv_ext[0:HALO, :] = glu(halo[:, POOL_W:POOL_W + CONV_W], halo[:, POOL_W + CONV_W:])
    conv_ext[HALO:HALO + tm, :] = glu(cur[:, POOL_W:POOL_W + CONV_W], cur[:, POOL_W + CONV_W:])

    lane = lax.broadcasted_iota(jnp.int32, (1, LANES), 1)
    low_half = lane < (LANES // 2)
    rc = min(ROW_CHUNK, tm)
    pooled_rows = []
    for r0 in range(0, tm, rc):
        pos = ti * tm + r0 + lax.broadcasted_iota(jnp.int32, (rc, 1), 0)
        halves = []
        for c, (w_lo, w_hi) in enumerate(((POOL_WINDOWS[0], POOL_WINDOWS[1]),
                                          (POOL_WINDOWS[2], POOL_WINDOWS[3]))):
            cols = slice(c * LANES, (c + 1) * LANES)
            base = HALO + r0
            ident = pool_ext[base:base + rc, cols]
            s_lo = ident
            for j in range(1, w_lo):
                s_lo = s_lo + pool_ext[base - j:base - j + rc, cols]
            s_hi = s_lo
            for j in range(w_lo, w_hi):
                s_hi = s_hi + pool_ext[base - j:base - j + rc, cols]
            total = jnp.where(low_half, s_lo, s_hi)
            win = jnp.where(low_half, w_lo, w_hi)
            count = jnp.minimum(pos + 1, win).astype(F32)
            halves.append(total / count - ident)
        pooled_rows.append(jnp.concatenate(halves, axis=1))
    pooled = jnp.concatenate(pooled_rows, axis=0).astype(BF16)
    o_b = _dot(pooled, wp_ref[...]) * ps_ref[...]

    conv_rows = []
    for r0 in range(0, tm, rc):
        part = jnp.zeros((rc, CONV_W), F32) + cb_ref[...]
        for j in range(CONV_K):
            off = HALO - (CONV_K - 1) + j + r0
            part = part + cw_ref[j:j + 1, :] * conv_ext[off:off + rc, :]
        conv_rows.append(part)
    conv = jnp.concatenate(conv_rows, axis=0)
    mu = jnp.mean(conv, axis=-1, keepdims=True)
    cen = conv - mu
    var = jnp.mean(cen * cen, axis=-1, keepdims=True)
    ln = cen * lax.rsqrt(var + LN_EPS) * lg_ref[...] + lb_ref[...]
    o_c = ln * (1.0 / (1.0 + jnp.exp(-ln)))

    gn = gn_ref[...]
    o = jnp.concatenate([
        _rms(oa_ref[0], gn[:, 0:ATTN_W]),
        _rms(o_b, gn[:, ATTN_W:ATTN_W + POOL_W]),
        _rms(o_c, gn[:, ATTN_W + POOL_W:]),
    ], axis=1).astype(BF16)
    out_ref[0] = x_ref[0] + _dot(o, wo_ref[...])


def _mixout(x3, rest3, oa3, wp_bd, pool_scale, conv_w, conv_b, ln_g, ln_b, gn_g, wo_bf16):
    b, s, d = x3.shape
    tm = min(TOKEN_TILE, s)
    hb = tm // HALO
    nr = rest3.shape[-1]
    row = lambda a: a.reshape(1, -1)
    return pl.pallas_call(
        _mixout_kernel,
        grid=(b, s // tm),
        in_specs=[pl.BlockSpec((1, tm, d), lambda bi, ti: (bi, ti, 0)),
                  pl.BlockSpec((1, tm, nr), lambda bi, ti: (bi, ti, 0)),
                  pl.BlockSpec((1, HALO, nr), lambda bi, ti: (bi, jnp.maximum(ti * hb - 1, 0), 0)),
                  pl.BlockSpec((1, tm, ATTN_W), lambda bi, ti: (bi, ti, 0)),
                  _const_spec((POOL_W, POOL_W)),
                  _const_spec((1, POOL_W)),
                  _const_spec((CONV_K, CONV_W)),
                  _const_spec((1, CONV_W)),
                  _const_spec((1, CONV_W)),
                  _const_spec((1, CONV_W)),
                  _const_spec((1, d)),
                  _const_spec((d, d))],
        out_specs=pl.BlockSpec((1, tm, d), lambda bi, ti: (bi, ti, 0)),
        out_shape=jax.ShapeDtypeStruct((b, s, d), F32),
        scratch_shapes=[pltpu.VMEM((HALO + tm, POOL_W), F32),
                        pltpu.VMEM((HALO + tm, CONV_W), F32)],
        compiler_params=_cparams(2),
        name="mixout",
    )(x3, rest3, rest3, oa3, wp_bd, row(pool_scale), conv_w, row(conv_b), row(ln_g), row(ln_b),
      row(gn_g), wo_bf16)


def _swiglu_rows(h_bf16, wg_ref, wu_ref, wd_ref, acc):
    f0 = 0
    for fc in FF_CHUNKS:
        gate = _dot(h_bf16, wg_ref[:, f0:f0 + fc])
        up = _dot(h_bf16, wu_ref[:, f0:f0 + fc])
        act = (gate * (1.0 / (1.0 + jnp.exp(-gate))) * up).astype(BF16)
        acc = acc + _dot(act, wd_ref[f0:f0 + fc, :])
        f0 += fc
    return acc


def _dense_ffn_kernel(x_ref, g_ref, wg_ref, wu_ref, wd_ref, fg_ref, out_ref, *, final_norm):
    x = x_ref[...]
    h = _rms(x, g_ref[...]).astype(BF16)
    y = _swiglu_rows(h, wg_ref, wu_ref, wd_ref, x)
    if final_norm:
        y = _rms(y, fg_ref[...])
    out_ref[...] = y


def _dense_ffn(x2, g, wg, wu, wd, final_g, final_norm):
    t, d = x2.shape
    f = wg.shape[1]
    assert sum(FF_CHUNKS) == f
    tm = min(TOKEN_TILE, t)
    return pl.pallas_call(
        functools.partial(_dense_ffn_kernel, final_norm=final_norm),
        grid=(t // tm,),
        in_specs=[pl.BlockSpec((tm, d), lambda i: (i, 0)),
                  _const_spec((1, d)),
                  _const_spec((d, f)),
                  _const_spec((d, f)),
                  _const_spec((f, d)),
                  _const_spec((1, d))],
        out_specs=pl.BlockSpec((tm, d), lambda i: (i, 0)),
        out_shape=jax.ShapeDtypeStruct((t, d), F32),
        compiler_params=_cparams(1),
        name="dense_ffn",
    )(x2, g.reshape(1, d), wg, wu, wd, final_g.reshape(1, d))


META_E0, META_E1, META_G0, META_G1, META_R0, META_R1 = range(6)


def _split3(a):
    p0 = a.astype(BF16)
    r1 = a - p0.astype(F32)
    p1 = r1.astype(BF16)
    p2 = (r1 - p1.astype(F32)).astype(BF16)
    return p0, p1, p2


def _router_kernel(x_ref, g_ref, wr_ref, meta_ref, counts_ref, run_ref):
    tm = x_ref.shape[0]
    i = pl.program_id(0)

    @pl.when(i == 0)
    def _():
        run_ref[...] = jnp.zeros_like(run_ref)

    h = _rms(x_ref[...], g_ref[...])
    hs = _split3(h)
    ws = _split3(wr_ref[...])
    logits = jnp.zeros((tm, LANES), F32)
    for a, b in ((2, 0), (1, 1), (0, 2), (1, 0), (0, 1), (0, 0)):
        logits = logits + _dot(hs[a], ws[b])

    lane = lax.broadcasted_iota(jnp.int32, (tm, LANES), 1).astype(F32)
    neg = jnp.float32(-jnp.inf)
    logits = jnp.where(lane < N_EXPERTS, logits, neg)
    v0 = jnp.max(logits, axis=-1, keepdims=True)
    e0 = jnp.min(jnp.where(logits == v0, lane, float(LANES)), axis=-1, keepdims=True)
    masked = jnp.where(lane == e0, neg, logits)
    v1 = jnp.max(masked, axis=-1, keepdims=True)
    e1 = jnp.min(jnp.where(masked == v1, lane, float(LANES)), axis=-1, keepdims=True)
    ex = jnp.exp(v1 - v0)
    g0 = 1.0 / (1.0 + ex)
    g1 = ex / (1.0 + ex)

    sel0 = lane == e0
    sel1 = lane == e1
    onehot = jnp.where(jnp.logical_or(sel0, sel1), 1.0, 0.0)
    rr = lax.broadcasted_iota(jnp.int32, (tm, tm), 0)
    cc = lax.broadcasted_iota(jnp.int32, (tm, tm), 1)
    before = jnp.where(cc < rr, 1.0, 0.0).astype(BF16)
    rank_all = _dot(before, onehot.astype(BF16)) + run_ref[0:1, :]
    r0 = jnp.sum(jnp.where(sel0, rank_all, 0.0), axis=-1, keepdims=True)
    r1 = jnp.sum(jnp.where(sel1, rank_all, 0.0), axis=-1, keepdims=True)
    new_run = run_ref[0:1, :] + jnp.sum(onehot, axis=0, keepdims=True)
    run_ref[...] = jnp.broadcast_to(new_run, run_ref.shape)
    counts_ref[...] = jnp.broadcast_to(new_run, counts_ref.shape)

    meta = jnp.zeros((tm, LANES), F32)
    for col, val in ((META_E0, e0), (META_E1, e1), (META_G0, g0),
                     (META_G1, g1), (META_R0, r0), (META_R1, r1)):
        meta = jnp.where(lane == col, val, meta)
    meta_ref[...] = meta


def _router(x2, g, wr_pad):
    t, d = x2.shape
    tm = min(TOKEN_TILE, t)
    return pl.pallas_call(
        _router_kernel,
        grid=(t // tm,),
        in_specs=[pl.BlockSpec((tm, d), lambda i: (i, 0)),
                  _const_spec((1, d)),
                  _const_spec((d, LANES))],
        out_specs=[pl.BlockSpec((tm, LANES), lambda i: (i, 0)),
                   pl.BlockSpec((SUBLANES, LANES), lambda i: (0, 0))],
        out_shape=[jax.ShapeDtypeStruct((t, LANES), F32),
                   jax.ShapeDtypeStruct((SUBLANES, LANES), F32)],
        scratch_shapes=[pltpu.VMEM((SUBLANES, LANES), F32)],
        compiler_params=_cparams(1),
        name="moe_router",
    )(x2, g.reshape(1, d), wr_pad)


ROWS_PER_ISSUE = 8


def _row_view(ref, row):
    return ref.at[pl.ds(pl.multiple_of(row * SUBLANES, SUBLANES), SUBLANES), :]


def _dispatch_kernel(pos_ref, pad_start_ref, pad_on_ref, x_ref, g_ref, xs_ref, rows, zeros, sems, zsem):
    tm = x_ref.shape[0]
    i = pl.program_id(0)
    n = pl.num_programs(0)
    slot = lax.rem(i, 2)
    tile_rows = tm * SUBLANES

    def drain(s):
        for _ in range(TOP_K):
            pltpu.make_async_copy(rows.at[s], xs_ref.at[pl.ds(0, tile_rows), :], sems.at[s]).wait()

    @pl.when(i == 0)
    def _():
        zeros[...] = jnp.zeros_like(zeros)
        zrows = zeros.shape[0]
        for e in range(pad_start_ref.shape[0]):
            @pl.when(pad_on_ref[e] > 0)
            def _():
                start = pl.multiple_of(pad_start_ref[e] * SUBLANES, SUBLANES)
                pltpu.make_async_copy(zeros, xs_ref.at[pl.ds(start, zrows), :], zsem).start()
        for e in range(pad_start_ref.shape[0]):
            @pl.when(pad_on_ref[e] > 0)
            def _():
                pltpu.make_async_copy(zeros, xs_ref.at[pl.ds(0, zrows), :], zsem).wait()

    @pl.when(i >= 2)
    def _():
        drain(slot)

    h = _rms(x_ref[...], g_ref[...])
    buf = rows.at[slot]
    for j in range(h.shape[1] // LANES):
        buf[pl.ds(j, tm, stride=SUBLANES), :] = h[:, j * LANES:(j + 1) * LANES]

    def issue(c, _):
        for u in range(ROWS_PER_ISSUE):
            r = c * ROWS_PER_ISSUE + u
            for k in range(TOP_K):
                dst = pos_ref[0, 0, r * TOP_K + k]
                pltpu.make_async_copy(_row_view(buf, r), _row_view(xs_ref, dst), sems.at[slot]).start()
        return 0

    lax.fori_loop(0, tm // ROWS_PER_ISSUE, issue, 0)

    @pl.when(i == n - 1)
    def _():
        @pl.when(n >= 2)
        def _():
            drain(1 - slot)
        drain(slot)


def _dispatch(x2, g, pos, pad_start, pad_on, n_rows):
    t, d = x2.shape
    tm = min(TOKEN_TILE, t)
    n_tiles = t // tm
    pos3 = pos.reshape(n_tiles, 1, tm * TOP_K)
    grid_spec = pltpu.PrefetchScalarGridSpec(
        num_scalar_prefetch=0,
        grid=(n_tiles,),
        in_specs=[pl.BlockSpec((1, 1, tm * TOP_K), lambda i: (i, 0, 0), memory_space=pltpu.SMEM),
                  pl.BlockSpec(memory_space=pltpu.SMEM),
                  pl.BlockSpec(memory_space=pltpu.SMEM),
                  pl.BlockSpec((tm, d), lambda i: (i, 0)),
                  _const_spec((1, d))],
        out_specs=pl.BlockSpec(memory_space=pl.ANY),
        scratch_shapes=[pltpu.VMEM((2, tm * SUBLANES, LANES), F32),
                        pltpu.VMEM((GROUP_TILE * SUBLANES, LANES), F32),
                        pltpu.SemaphoreType.DMA((2,)),
                        pltpu.SemaphoreType.DMA(())],
    )
    return pl.pallas_call(
        _dispatch_kernel,
        grid_spec=grid_spec,
        out_shape=jax.ShapeDtypeStruct((n_rows * SUBLANES, LANES), F32),
        compiler_params=_cparams(1),
        name="moe_dispatch",
    )(pos3, pad_start, pad_on, x2, g.reshape(1, d))


def _group_ffn_kernel(te_ref, last_ref, xs_ref, wg_ref, wu_ref, wd_ref, ys_ref):
    i = pl.program_id(0)
    tg = xs_ref.shape[0] // SUBLANES
    d = wg_ref.shape[1]

    @pl.when(i <= last_ref[0])
    def _():
        x = jnp.concatenate([xs_ref[pl.ds(j, tg, stride=SUBLANES), :] for j in range(d // LANES)],
                            axis=1).astype(BF16)
        y = _swiglu_rows(x, wg_ref.at[0], wu_ref.at[0], wd_ref.at[0], jnp.zeros((tg, d), F32))
        for j in range(d // LANES):
            ys_ref[pl.ds(j, tg, stride=SUBLANES), :] = y[:, j * LANES:(j + 1) * LANES]

    @pl.when(i > last_ref[0])
    def _():
        ys_ref[...] = jnp.zeros_like(ys_ref)


def _group_ffn(xs, tile_expert, last_tile, wg, wu, wd):
    n_tiles = tile_expert.shape[0]
    _, d, f = wg.shape
    blk = GROUP_TILE * SUBLANES
    grid_spec = pltpu.PrefetchScalarGridSpec(
        num_scalar_prefetch=2,
        grid=(n_tiles,),
        in_specs=[pl.BlockSpec((blk, LANES), lambda i, te, last: (jnp.minimum(i, last[0]), 0)),
                  pl.BlockSpec((1, d, f), lambda i, te, last: (te[i], 0, 0)),
                  pl.BlockSpec((1, d, f), lambda i, te, last: (te[i], 0, 0)),
                  pl.BlockSpec((1, f, d), lambda i, te, last: (te[i], 0, 0))],
        out_specs=pl.BlockSpec((blk, LANES), lambda i, te, last: (i, 0)),
    )
    return pl.pallas_call(
        _group_ffn_kernel,
        grid_spec=grid_spec,
        out_shape=jax.ShapeDtypeStruct(xs.shape, F32),
        compiler_params=_cparams(1),
        name="moe_group_ffn",
    )(tile_expert, last_tile, xs, wg, wu, wd)


def _combine_kernel(pos_ref, pos_next_ref, x_ref, meta_ref, fg_ref, ys_ref, out_ref, bufs, sems, *,
                    final_norm):
    tm = x_ref.shape[0]
    i = pl.program_id(0)
    n = pl.num_programs(0)
    slot = lax.rem(i, 2)
    tile_rows = tm * SUBLANES

    def fetch(p_ref, s):
        def issue(c, _):
            for u in range(ROWS_PER_ISSUE):
                r = c * ROWS_PER_ISSUE + u
                for k in range(TOP_K):
                    src = p_ref[0, 0, r * TOP_K + k]
                    pltpu.make_async_copy(_row_view(ys_ref, src), _row_view(bufs.at[s, k], r),
                                          sems.at[s]).start()
            return 0
        lax.fori_loop(0, tm // ROWS_PER_ISSUE, issue, 0)

    @pl.when(i == 0)
    def _():
        fetch(pos_ref, 0)

    @pl.when(i + 1 < n)
    def _():
        fetch(pos_next_ref, 1 - slot)

    for k in range(TOP_K):
        pltpu.make_async_copy(ys_ref.at[pl.ds(0, tile_rows), :], bufs.at[slot, k], sems.at[slot]).wait()

    meta = meta_ref[...]
    g0 = meta[:, META_G0:META_G0 + 1]
    g1 = meta[:, META_G1:META_G1 + 1]
    x = x_ref[...]
    cols = []
    for j in range(x.shape[1] // LANES):
        y0 = bufs[slot, 0, pl.ds(j, tm, stride=SUBLANES), :]
        y1 = bufs[slot, 1, pl.ds(j, tm, stride=SUBLANES), :]
        cols.append(x[:, j * LANES:(j + 1) * LANES] + (g0 * y0 + g1 * y1))
    y = jnp.concatenate(cols, axis=1)
    if final_norm:
        y = _rms(y, fg_ref[...])
    out_ref[...] = y


def _combine(x2, meta, pos, ys, final_g, final_norm):
    t, d = x2.shape
    tm = min(GROUP_TILE, t)
    n_tiles = t // tm
    pos3 = pos.reshape(n_tiles, 1, tm * TOP_K)
    grid_spec = pltpu.PrefetchScalarGridSpec(
        num_scalar_prefetch=0,
        grid=(n_tiles,),
        in_specs=[pl.BlockSpec((1, 1, tm * TOP_K), lambda i: (i, 0, 0), memory_space=pltpu.SMEM),
                  pl.BlockSpec((1, 1, tm * TOP_K), lambda i: (jnp.minimum(i + 1, n_tiles - 1), 0, 0),
                               memory_space=pltpu.SMEM),
                  pl.BlockSpec((tm, d), lambda i: (i, 0)),
                  pl.BlockSpec((tm, LANES), lambda i: (i, 0)),
                  _const_spec((1, d)),
                  pl.BlockSpec(memory_space=pl.ANY)],
        out_specs=pl.BlockSpec((tm, d), lambda i: (i, 0)),
        scratch_shapes=[pltpu.VMEM((2, TOP_K, tm * SUBLANES, LANES), F32),
                        pltpu.SemaphoreType.DMA((2,))],
    )
    return pl.pallas_call(
        functools.partial(_combine_kernel, final_norm=final_norm),
        grid_spec=grid_spec,
        out_shape=jax.ShapeDtypeStruct((t, d), F32),
        compiler_params=_cparams(1),
        name="moe_combine",
    )(pos3, pos3, x2, meta, final_g.reshape(1, d), ys)


def _moe_ffn(x2, g, wr, wg, wu, wd, final_g, final_norm):
    t, d = x2.shape
    wr_pad = jnp.zeros((d, LANES), F32).at[:, :N_EXPERTS].set(wr)
    meta, counts = _router(x2, g, wr_pad)

    cnt = counts[0, :N_EXPERTS].astype(jnp.int32)
    padded = ((cnt + GROUP_TILE - 1) // GROUP_TILE) * GROUP_TILE
    ends = jnp.cumsum(padded)
    offs = ends - padded
    n_tiles = (t * TOP_K) // GROUP_TILE + N_EXPERTS
    n_rows = n_tiles * GROUP_TILE
    eid = meta[:, META_E0:META_E1 + 1].astype(jnp.int32)
    rank = meta[:, META_R0:META_R1 + 1].astype(jnp.int32)
    pos = (offs[eid] + rank).reshape(-1)
    tail_tiles = ends[-1] + jnp.arange(N_EXPERTS, dtype=jnp.int32) * GROUP_TILE
    pad_start = jnp.concatenate([ends - GROUP_TILE, tail_tiles])
    pad_on = jnp.concatenate([cnt > 0, tail_tiles < n_rows]).astype(jnp.int32)
    pad_start = jnp.clip(pad_start, 0, n_rows - GROUP_TILE).astype(jnp.int32)
    tile_start = jnp.arange(n_tiles, dtype=jnp.int32) * GROUP_TILE
    last_tile = jnp.maximum(ends[-1] // GROUP_TILE - 1, 0).astype(jnp.int32).reshape(1)
    tile_expert = jnp.sum(tile_start[:, None] >= ends[None, :], axis=1).astype(jnp.int32)
    tile_expert = jnp.minimum(tile_expert, tile_expert[last_tile[0]])

    xs = _dispatch(x2, g, pos, pad_start, pad_on, n_rows)
    ys = _group_ffn(xs, tile_expert, last_tile, wg, wu, wd)
    return _combine(x2, meta, pos, ys, final_g, final_norm)


def kernel(x, attn_norm_g, w_in, pool_w, pool_scale, conv_w, conv_b, conv_ln_g, conv_ln_b,
           group_norm_g, w_out, ffn_norm_g, dense_w_gate, dense_w_up, dense_w_down,
           router_w, moe_w_gate, moe_w_up, moe_w_down, final_norm_g):
    b, s, d = x.shape
    depth = w_in.shape[0]
    t = b * s
    for l in range(depth):
        x2 = x.reshape(t, d)
        qkv, rest = _inproj(x2, attn_norm_g[l], w_in[l].astype(BF16))
        o_a = _attention(qkv.reshape(b, s, -1))
        wp_bd = jax.scipy.linalg.block_diag(*[pool_w[l, gi] for gi in range(pool_w.shape[1])]).astype(BF16)
        x = _mixout(x, rest.reshape(b, s, -1), o_a, wp_bd, pool_scale[l], conv_w[l], conv_b[l],
                    conv_ln_g[l], conv_ln_b[l], group_norm_g[l], w_out[l].astype(BF16))
        x2 = x.reshape(t, d)
        last = l == depth - 1
        i = l // 2
        if l % 2 == 0:
            x2 = _dense_ffn(x2, ffn_norm_g[l], dense_w_gate[i].astype(BF16), dense_w_up[i].astype(BF16),
                            dense_w_down[i].astype(BF16), final_norm_g, last)
        else:
            x2 = _moe_ffn(x2, ffn_norm_g[l], router_w[i], moe_w_gate[i].astype(BF16),
                          moe_w_up[i].astype(BF16), moe_w_down[i].astype(BF16), final_norm_g, last)
        x = x2.reshape(b, s, d)
    return x
```

```python
import functools
import math

import jax
import jax.numpy as jnp
from jax import lax
from jax.experimental import pallas as pl
from jax.experimental.pallas import tpu as pltpu

F32 = jnp.float32
BF16 = jnp.bfloat16

LANES = 128
SUBLANES = 8

N_HEADS = 8
HEAD_DIM = 64
ATTN_W = N_HEADS * HEAD_DIM
POOL_W = 256
CONV_W = 256
POOL_WINDOWS = (2, 4, 8, 16)
CONV_K = 31
HALO = 32
N_EXPERTS = 8
TOP_K = 2
RMS_EPS = 1e-6
LN_EPS = 1e-5

ATTN_BLOCK = 256
EXP_ZERO_BELOW = -105.0

TOKEN_TILE = 512
ROW_CHUNK = 128
GROUP_TILE = 256
FF_CHUNKS = (1024, 1024, 768)

VMEM_LIMIT = 56 * 1024 * 1024


def _cparams(n_axes, vmem=VMEM_LIMIT):
    return pltpu.CompilerParams(dimension_semantics=("arbitrary",) * n_axes, vmem_limit_bytes=vmem)


def _rms(x, g):
    return x * lax.rsqrt(jnp.mean(x * x, axis=-1, keepdims=True) + RMS_EPS) * g


def _dot(a, b):
    return jnp.dot(a, b, preferred_element_type=F32)


def _const_spec(shape):
    zeros = (0,) * len(shape)
    return pl.BlockSpec(shape, lambda *_: zeros, pipeline_mode=pl.Buffered(1))


def _inproj_kernel(x_ref, g_ref, w_ref, qkv_ref, rest_ref):
    h = _rms(x_ref[...], g_ref[...]).astype(BF16)
    nq = qkv_ref.shape[-1]
    qkv_ref[...] = _dot(h, w_ref[:, :nq]).astype(BF16)
    rest_ref[...] = _dot(h, w_ref[:, nq:])


def _inproj(x2, g, w_bf16):
    t, d = x2.shape
    n = w_bf16.shape[1]
    nq = 3 * ATTN_W
    tm = min(TOKEN_TILE, t)
    return pl.pallas_call(
        _inproj_kernel,
        grid=(t // tm,),
        in_specs=[pl.BlockSpec((tm, d), lambda i: (i, 0)),
                  _const_spec((1, d)),
                  _const_spec((d, n))],
        out_specs=[pl.BlockSpec((tm, nq), lambda i: (i, 0)),
                   pl.BlockSpec((tm, n - nq), lambda i: (i, 0))],
        out_shape=[jax.ShapeDtypeStruct((t, nq), BF16),
                   jax.ShapeDtypeStruct((t, n - nq), F32)],
        compiler_params=_cparams(1),
        name="inproj",
    )(x2, g.reshape(1, d), w_bf16)


def _attn_kernel(q_ref, k_ref, v_ref, o_ref, carry_ref):
    blk = q_ref.shape[1]
    n_pairs = q_ref.shape[2] // LANES
    qi = pl.program_id(1)
    scale = jnp.asarray(1.0 / math.sqrt(HEAD_DIM), BF16)

    lane = lax.broadcasted_iota(jnp.int32, (1, LANES), 1)
    head_lanes = (lane < HEAD_DIM, lane >= HEAD_DIM)
    zero_bf = jnp.zeros((), BF16)

    jj = lax.broadcasted_iota(jnp.int32, (2 * blk, blk), 0)
    ss = lax.broadcasted_iota(jnp.int32, (2 * blk, blk), 1)
    upper2 = jnp.where(jnp.where(jj >= blk, jj - blk, jj) > ss, -1.0, 0.0).astype(BF16)

    n_heads = 2 * n_pairs

    def one_block(j, diagonal):
        start = pl.multiple_of(j * blk, blk)
        if diagonal:
            r_idx = lax.broadcasted_iota(jnp.int32, (blk, blk), 0)
            c_idx = lax.broadcasted_iota(jnp.int32, (blk, blk), 1)
            causal = c_idx < r_idx

        def cols(n):
            return slice((n // 2) * LANES, (n // 2 + 1) * LANES)

        def scores(n):
            q = q_ref[0, :, cols(n)] * scale
            k = k_ref[0, pl.ds(start, blk), cols(n)]
            qh = jnp.where(head_lanes[n % 2], q, zero_bf)
            return lax.dot_general(qh, k, (((1,), (1,)), ((), ())), preferred_element_type=F32)

        def log_terms(z):
            softplus = jnp.maximum(z, 0.0) + jnp.log(1.0 + jnp.exp(jnp.minimum(z, -z)))
            if diagonal:
                softplus = jnp.where(causal, softplus, 0.0)
            hi = softplus.astype(BF16)
            lo = (softplus - hi.astype(F32)).astype(BF16)
            return z - softplus, jnp.concatenate([hi, lo], axis=1), softplus[:, 0:1]

        def weighted_values(n, log_beta, rest, first_col):
            arg = log_beta + rest
            if not diagonal:
                arg = arg + carry_ref[n]
            w = jnp.exp(arg)
            if diagonal:
                w = jnp.where(causal, w, 0.0)
            v = v_ref[0, pl.ds(start, blk), cols(n)]
            vh = jnp.where(head_lanes[n % 2], v, zero_bf)
            block_sum = rest[:, 0:1] - first_col
            carry = block_sum if diagonal else carry_ref[n] + block_sum
            carry_ref[n] = carry
            return _dot(w.astype(BF16), vh), carry

        z, terms, rest, pv, bound = {}, {}, {}, {}, None
        for step in range(n_heads + 3):
            n4, n3, n2, n1 = step - 3, step - 2, step - 1, step
            if 0 <= n3 < n_heads:
                rest[n3] = _dot(terms[n3][1], upper2)
            if 0 <= n1 < n_heads:
                z[n1] = scores(n1)
            if 0 <= n4 < n_heads:
                pv[n4], carry = weighted_values(n4, terms[n4][0], rest.pop(n4), terms[n4][2])
                del terms[n4]
                bound = carry if bound is None else jnp.maximum(bound, carry)
                if n4 % 2 == 1:
                    both = pv.pop(n4 - 1) + pv.pop(n4)
                    o_ref[0, :, cols(n4)] = both if diagonal else o_ref[0, :, cols(n4)] + both
            if 0 <= n2 < n_heads:
                terms[n2] = log_terms(z.pop(n2))
        return (jnp.max(bound) > EXP_ZERO_BELOW).astype(jnp.int32)

    go = one_block(qi, True)

    def cond(state):
        return jnp.logical_and(state[0] >= 0, state[1] > 0)

    def body(state):
        j = state[0]
        return j - 1, one_block(j, False)

    lax.while_loop(cond, body, (qi - 1, go))


def _attn_and_cast_kernel(*refs, n_cast):
    q_ref, k_ref, v_ref = refs[:3]
    srcs = refs[3:3 + n_cast]
    o_ref = refs[3 + n_cast]
    dsts = refs[4 + n_cast:4 + 2 * n_cast]
    carry_ref = refs[4 + 2 * n_cast]
    for src, dst in zip(srcs, dsts):
        dst[...] = src[...].astype(BF16)
    _attn_kernel(q_ref, k_ref, v_ref, o_ref, carry_ref)


def _attention(qkv3, cast_weights=(), cast_index=0):
    b, s, _ = qkv3.shape
    blk = min(ATTN_BLOCK, s)
    n_q = s // blk
    steps = b * n_q
    cast_2d = [w.reshape(-1, w.shape[-1]) for w in cast_weights]
    in_cast, out_cast, out_shapes = [], [], []
    for w, w2 in zip(cast_weights, cast_2d):
        layer_rows = w.shape[1] * w.shape[2]
        rows = layer_rows // steps
        assert rows * steps == layer_rows and rows % 16 == 0, w.shape
        in_cast.append(pl.BlockSpec((rows, w2.shape[1]),
                                    lambda bi, qi: (cast_index * steps + bi * n_q + qi, 0)))
        out_cast.append(pl.BlockSpec((rows, w2.shape[1]), lambda bi, qi: (bi * n_q + qi, 0)))
        out_shapes.append(jax.ShapeDtypeStruct((layer_rows, w2.shape[1]), BF16))
    outs = pl.pallas_call(
        functools.partial(_attn_and_cast_kernel, n_cast=len(cast_2d)),
        grid=(b, n_q),
        in_specs=[pl.BlockSpec((1, blk, ATTN_W), lambda bi, qi: (bi, qi, 0)),
                  pl.BlockSpec((1, s, ATTN_W), lambda bi, qi: (bi, 0, 1)),
                  pl.BlockSpec((1, s, ATTN_W), lambda bi, qi: (bi, 0, 2))] + in_cast,
        out_specs=[pl.BlockSpec((1, blk, ATTN_W), lambda bi, qi: (bi, qi, 0))] + out_cast,
        out_shape=[jax.ShapeDtypeStruct((b, s, ATTN_W), F32)] + out_shapes,
        scratch_shapes=[pltpu.VMEM((N_HEADS, blk, 1), F32)],
        compiler_params=_cparams(2),
        name="sb_attention",
    )(qkv3, qkv3, qkv3, *cast_2d)
    return outs[0], [o.reshape(w.shape[1:]) for o, w in zip(outs[1:], cast_weights)]


def _mixout_kernel(x_ref, rest_ref, halo_ref, oa_ref, wp_ref, ps_ref, cw_ref, cb_ref,
                   lg_ref, lb_ref, gn_ref, wo_ref, out_ref, pool_ext, conv_ext, conv_shift):
    tm = x_ref.shape[1]
    ti = pl.program_id(1)
    has_history = (ti > 0).astype(F32)

    def glu(val, gate):
        return val * (1.0 / (1.0 + jnp.exp(-gate)))

    halo = halo_ref[0] * has_history
    cur = rest_ref[0]
    pool_ext[0:HALO, :] = halo[:, 0:POOL_W]
    pool_ext[HALO:HALO + tm, :] = cur[:, 0:POOL_W]
    conv_ext[0:HALO, :] = glu(halo[:, POOL_W:POOL_W + CONV_W], halo[:, POOL_W + CONV_W:])
    conv_ext[HALO:HALO + tm, :] = glu(cur[:, POOL_W:POOL_W + CONV_W], cur[:, POOL_W + CONV_W:])

    lane = lax.broadcasted_iota(jnp.int32, (1, LANES), 1)
    low_half = lane < (LANES // 2)
    rc = min(ROW_CHUNK, tm)
    pooled_rows = []
    for r0 in range(0, tm, rc):
        pos = ti * tm + r0 + lax.broadcasted_iota(jnp.int32, (rc, 1), 0)
        halves = []
        for c, (w_lo, w_hi) in enumerate(((POOL_WINDOWS[0], POOL_WINDOWS[1]),
                                          (POOL_WINDOWS[2], POOL_WINDOWS[3]))):
            cols = slice(c * LANES, (c + 1) * LANES)
            base = HALO + r0
            ident = pool_ext[base:base + rc, cols]
            s_lo = ident
            for j in range(1, w_lo):
                s_lo = s_lo + pool_ext[base - j:base - j + rc, cols]
            s_hi = s_lo
            for j in range(w_lo, w_hi):
                s_hi = s_hi + pool_ext[base - j:base - j + rc, cols]
            total = jnp.where(low_half, s_lo, s_hi)
            win = jnp.where(low_half, w_lo, w_hi)
            count = jnp.minimum(pos + 1, win).astype(F32)
            halves.append(total / count - ident)
        pooled_rows.append(jnp.concatenate(halves, axis=1))
    pooled = jnp.concatenate(pooled_rows, axis=0).astype(BF16)
    o_b = _dot(pooled, wp_ref[...]) * ps_ref[...]

    ext_rows = HALO + tm
    conv_ext[ext_rows:ext_rows + SUBLANES, :] = jnp.zeros((SUBLANES, CONV_W), F32)
    for s in range(1, SUBLANES):
        conv_shift[s - 1] = conv_ext[s:s + ext_rows, :]
    conv_rows = []
    for r0 in range(0, tm, rc):
        part = jnp.zeros((rc, CONV_W), F32) + cb_ref[...]
        for j in range(CONV_K):
            off = HALO - (CONV_K - 1) + j + r0
            s, aligned = off % SUBLANES, off - off % SUBLANES
            window = (conv_ext[aligned:aligned + rc, :] if s == 0
                      else conv_shift[s - 1, aligned:aligned + rc, :])
            part = part + cw_ref[j:j + 1, :] * window
        conv_rows.append(part)
    conv = jnp.concatenate(conv_rows, axis=0)
    mu = jnp.mean(conv, axis=-1, keepdims=True)
    cen = conv - mu
    var = jnp.mean(cen * cen, axis=-1, keepdims=True)
    ln = cen * lax.rsqrt(var + LN_EPS) * lg_ref[...] + lb_ref[...]
    o_c = ln * (1.0 / (1.0 + jnp.exp(-ln)))

    gn = gn_ref[...]
    o = jnp.concatenate([
        _rms(oa_ref[0], gn[:, 0:ATTN_W]),
        _rms(o_b, gn[:, ATTN_W:ATTN_W + POOL_W]),
        _rms(o_c, gn[:, ATTN_W + POOL_W:]),
    ], axis=1).astype(BF16)
    out_ref[0] = x_ref[0] + _dot(o, wo_ref[...])


def _mixout(x3, rest3, oa3, wp_bd, pool_scale, conv_w, conv_b, ln_g, ln_b, gn_g, wo_bf16):
    b, s, d = x3.shape
    tm = min(TOKEN_TILE, s)
    hb = tm // HALO
    nr = rest3.shape[-1]
    row = lambda a: a.reshape(1, -1)
    return pl.pallas_call(
        _mixout_kernel,
        grid=(b, s // tm),
        in_specs=[pl.BlockSpec((1, tm, d), lambda bi, ti: (bi, ti, 0)),
                  pl.BlockSpec((1, tm, nr), lambda bi, ti: (bi, ti, 0)),
                  pl.BlockSpec((1, HALO, nr), lambda bi, ti: (bi, jnp.maximum(ti * hb - 1, 0), 0)),
                  pl.BlockSpec((1, tm, ATTN_W), lambda bi, ti: (bi, ti, 0)),
                  _const_spec((POOL_W, POOL_W)),
                  _const_spec((1, POOL_W)),
                  _const_spec((CONV_K, CONV_W)),
                  _const_spec((1, CONV_W)),
                  _const_spec((1, CONV_W)),
                  _const_spec((1, CONV_W)),
                  _const_spec((1, d)),
                  _const_spec((d, d))],
        out_specs=pl.BlockSpec((1, tm, d), lambda bi, ti: (bi, ti, 0)),
        out_shape=jax.ShapeDtypeStruct((b, s, d), F32),
        scratch_shapes=[pltpu.VMEM((HALO + tm, POOL_W), F32),
                        pltpu.VMEM((HALO + tm + SUBLANES, CONV_W), F32),
                        pltpu.VMEM((SUBLANES - 1, HALO + tm, CONV_W), F32)],
        compiler_params=_cparams(2),
        name="mixout",
    )(x3, rest3, rest3, oa3, wp_bd, row(pool_scale), conv_w, row(conv_b), row(ln_g), row(ln_b),
      row(gn_g), wo_bf16)


def _swiglu_rows(h_bf16, wg_ref, wu_ref, wd_ref, acc):
    f0 = 0
    for fc in FF_CHUNKS:
        gate = _dot(h_bf16, wg_ref[:, f0:f0 + fc])
        up = _dot(h_bf16, wu_ref[:, f0:f0 + fc])
        act = (gate * (1.0 / (1.0 + jnp.exp(-gate))) * up).astype(BF16)
        acc = acc + _dot(act, wd_ref[f0:f0 + fc, :])
        f0 += fc
    return acc


def _dense_ffn_kernel(x_ref, g_ref, wg_ref, wu_ref, wd_ref, fg_ref, out_ref, *, final_norm):
    x = x_ref[...]
    h = _rms(x, g_ref[...]).astype(BF16)
    y = _swiglu_rows(h, wg_ref, wu_ref, wd_ref, x)
    if final_norm:
        y = _rms(y, fg_ref[...])
    out_ref[...] = y


def _dense_ffn(x2, g, wg, wu, wd, final_g, final_norm):
    t, d = x2.shape
    f = wg.shape[1]
    assert sum(FF_CHUNKS) == f
    tm = min(TOKEN_TILE, t)
    return pl.pallas_call(
        functools.partial(_dense_ffn_kernel, final_norm=final_norm),
        grid=(t // tm,),
        in_specs=[pl.BlockSpec((tm, d), lambda i: (i, 0)),
                  _const_spec((1, d)),
                  _const_spec((d, f)),
                  _const_spec((d, f)),
                  _const_spec((f, d)),
                  _const_spec((1, d))],
        out_specs=pl.BlockSpec((tm, d), lambda i: (i, 0)),
        out_shape=jax.ShapeDtypeStruct((t, d), F32),
        compiler_params=_cparams(1),
        name="dense_ffn",
    )(x2, g.reshape(1, d), wg, wu, wd, final_g.reshape(1, d))


META_E0, META_E1, META_G0, META_G1, META_R0, META_R1 = range(6)


def _split3(a):
    p0 = a.astype(BF16)
    r1 = a - p0.astype(F32)
    p1 = r1.astype(BF16)
    p2 = (r1 - p1.astype(F32)).astype(BF16)
    return p0, p1, p2


def _router_kernel(x_ref, g_ref, wrt_ref, meta_ref, counts_ref, run_ref):
    tm = x_ref.shape[0]
    i = pl.program_id(0)
    nt_dims = (((1,), (1,)), ((), ()))

    @pl.when(i == 0)
    def _():
        run_ref[...] = jnp.zeros_like(run_ref)

    h = _rms(x_ref[...], g_ref[...])
    hs = _split3(h)
    ws = _split3(wrt_ref[...])
    logits = jnp.zeros((N_EXPERTS, tm), F32)
    for a, b in ((2, 0), (1, 1), (0, 2), (1, 0), (0, 1), (0, 0)):
        logits = logits + lax.dot_general(ws[b], hs[a], nt_dims, preferred_element_type=F32)

    eidx = lax.broadcasted_iota(jnp.int32, (N_EXPERTS, tm), 0).astype(F32)
    neg = jnp.float32(-jnp.inf)
    v0 = jnp.max(logits, axis=0, keepdims=True)
    e0 = jnp.min(jnp.where(logits == v0, eidx, float(N_EXPERTS)), axis=0, keepdims=True)
    masked = jnp.where(eidx == e0, neg, logits)
    v1 = jnp.max(masked, axis=0, keepdims=True)
    e1 = jnp.min(jnp.where(masked == v1, eidx, float(N_EXPERTS)), axis=0, keepdims=True)
    ex = jnp.exp(v1 - v0)
    g0 = 1.0 / (1.0 + ex)
    g1 = ex / (1.0 + ex)

    sel0 = eidx == e0
    sel1 = eidx == e1
    onehot = jnp.where(jnp.logical_or(sel0, sel1), 1.0, 0.0)
    rr = lax.broadcasted_iota(jnp.int32, (tm, tm), 0)
    cc = lax.broadcasted_iota(jnp.int32, (tm, tm), 1)
    earlier = jnp.where(rr < cc, 1.0, 0.0).astype(BF16)
    run = run_ref[:, 0:1]
    rank_all = _dot(onehot.astype(BF16), earlier) + run
    r0 = jnp.sum(jnp.where(sel0, rank_all, 0.0), axis=0, keepdims=True)
    r1 = jnp.sum(jnp.where(sel1, rank_all, 0.0), axis=0, keepdims=True)
    new_run = run + jnp.sum(onehot, axis=1, keepdims=True)
    run_ref[...] = jnp.broadcast_to(new_run, run_ref.shape)
    counts_ref[...] = jnp.broadcast_to(new_run, counts_ref.shape)

    zero_row = jnp.zeros((1, tm), F32)
    meta_ref[...] = jnp.concatenate([e0, e1, g0, g1, r0, r1, zero_row, zero_row], axis=0)


def _router(x2, g, wr_t):
    t, d = x2.shape
    tm = min(TOKEN_TILE, t)
    return pl.pallas_call(
        _router_kernel,
        grid=(t // tm,),
        in_specs=[pl.BlockSpec((tm, d), lambda i: (i, 0)),
                  _const_spec((1, d)),
                  _const_spec((N_EXPERTS, d))],
        out_specs=[pl.BlockSpec((SUBLANES, tm), lambda i: (0, i)),
                   pl.BlockSpec((N_EXPERTS, LANES), lambda i: (0, 0))],
        out_shape=[jax.ShapeDtypeStruct((SUBLANES, t), F32),
                   jax.ShapeDtypeStruct((N_EXPERTS, LANES), F32)],
        scratch_shapes=[pltpu.VMEM((N_EXPERTS, LANES), F32)],
        compiler_params=_cparams(1),
        name="moe_router",
    )(x2, g.reshape(1, d), wr_t)


ROWS_PER_ISSUE = 8


def _row_view(ref, row):
    return ref.at[pl.ds(pl.multiple_of(row * SUBLANES, SUBLANES), SUBLANES), :]


def _dispatch_kernel(pos_ref, pad_start_ref, pad_on_ref, x_ref, g_ref, xs_ref, rows, zeros, sems, zsem):
    tm = x_ref.shape[0]
    i = pl.program_id(0)
    n = pl.num_programs(0)
    slot = lax.rem(i, 2)
    tile_rows = tm * SUBLANES

    def drain(s):
        for _ in range(TOP_K):
            pltpu.make_async_copy(rows.at[s], xs_ref.at[pl.ds(0, tile_rows), :], sems.at[s]).wait()

    @pl.when(i == 0)
    def _():
        zeros[...] = jnp.zeros_like(zeros)
        zrows = zeros.shape[0]
        for e in range(pad_start_ref.shape[0]):
            @pl.when(pad_on_ref[e] > 0)
            def _():
                start = pl.multiple_of(pad_start_ref[e] * SUBLANES, SUBLANES)
                pltpu.make_async_copy(zeros, xs_ref.at[pl.ds(start, zrows), :], zsem).start()
        for e in range(pad_start_ref.shape[0]):
            @pl.when(pad_on_ref[e] > 0)
            def _():
                pltpu.make_async_copy(zeros, xs_ref.at[pl.ds(0, zrows), :], zsem).wait()

    @pl.when(i >= 2)
    def _():
        drain(slot)

    h = _rms(x_ref[...], g_ref[...])
    buf = rows.at[slot]
    for j in range(h.shape[1] // LANES):
        buf[pl.ds(j, tm, stride=SUBLANES), :] = h[:, j * LANES:(j + 1) * LANES]

    def issue(c, _):
        for u in range(ROWS_PER_ISSUE):
            r = c * ROWS_PER_ISSUE + u
            for k in range(TOP_K):
                dst = pos_ref[0, 0, r * TOP_K + k]
                pltpu.make_async_copy(_row_view(buf, r), _row_view(xs_ref, dst),
                                      sems.at[slot]).start(priority=k)
        return 0

    lax.fori_loop(0, tm // ROWS_PER_ISSUE, issue, 0)

    @pl.when(i == n - 1)
    def _():
        @pl.when(n >= 2)
        def _():
            drain(1 - slot)
        drain(slot)


def _dispatch(x2, g, pos, pad_start, pad_on, n_rows):
    t, d = x2.shape
    tm = min(TOKEN_TILE, t)
    n_tiles = t // tm
    pos3 = pos.reshape(n_tiles, 1, tm * TOP_K)
    grid_spec = pltpu.PrefetchScalarGridSpec(
        num_scalar_prefetch=0,
        grid=(n_tiles,),
        in_specs=[pl.BlockSpec((1, 1, tm * TOP_K), lambda i: (i, 0, 0), memory_space=pltpu.SMEM),
                  pl.BlockSpec(memory_space=pltpu.SMEM),
                  pl.BlockSpec(memory_space=pltpu.SMEM),
                  pl.BlockSpec((tm, d), lambda i: (i, 0)),
                  _const_spec((1, d))],
        out_specs=pl.BlockSpec(memory_space=pl.ANY),
        scratch_shapes=[pltpu.VMEM((2, tm * SUBLANES, LANES), F32),
                        pltpu.VMEM((GROUP_TILE * SUBLANES, LANES), F32),
                        pltpu.SemaphoreType.DMA((2,)),
                        pltpu.SemaphoreType.DMA(())],
    )
    return pl.pallas_call(
        _dispatch_kernel,
        grid_spec=grid_spec,
        out_shape=jax.ShapeDtypeStruct((n_rows * SUBLANES, LANES), F32),
        compiler_params=_cparams(1),
        name="moe_dispatch",
    )(pos3, pad_start, pad_on, x2, g.reshape(1, d))


def _group_ffn_kernel(te_ref, last_ref, xs_ref, wg_ref, wu_ref, wd_ref, ys_ref):
    i = pl.program_id(0)
    tg = xs_ref.shape[0] // SUBLANES
    d = wg_ref.shape[1]

    @pl.when(i <= last_ref[0])
    def _():
        x = jnp.concatenate([xs_ref[pl.ds(j, tg, stride=SUBLANES), :] for j in range(d // LANES)],
                            axis=1).astype(BF16)
        y = _swiglu_rows(x, wg_ref.at[0], wu_ref.at[0], wd_ref.at[0], jnp.zeros((tg, d), F32))
        for j in range(d // LANES):
            ys_ref[pl.ds(j, tg, stride=SUBLANES), :] = y[:, j * LANES:(j + 1) * LANES]

    @pl.when(i > last_ref[0])
    def _():
        ys_ref[...] = jnp.zeros_like(ys_ref)


def _group_ffn(xs, tile_expert, last_tile, wg, wu, wd):
    n_tiles = tile_expert.shape[0]
    _, d, f = wg.shape
    blk = GROUP_TILE * SUBLANES
    grid_spec = pltpu.PrefetchScalarGridSpec(
        num_scalar_prefetch=2,
        grid=(n_tiles,),
        in_specs=[pl.BlockSpec((blk, LANES), lambda i, te, last: (jnp.minimum(i, last[0]), 0)),
                  pl.BlockSpec((1, d, f), lambda i, te, last: (te[i], 0, 0)),
                  pl.BlockSpec((1, d, f), lambda i, te, last: (te[i], 0, 0)),
                  pl.BlockSpec((1, f, d), lambda i, te, last: (te[i], 0, 0))],
        out_specs=pl.BlockSpec((blk, LANES), lambda i, te, last: (i, 0)),
    )
    return pl.pallas_call(
        _group_ffn_kernel,
        grid_spec=grid_spec,
        out_shape=jax.ShapeDtypeStruct(xs.shape, F32),
        compiler_params=_cparams(1),
        name="moe_group_ffn",
    )(tile_expert, last_tile, xs, wg, wu, wd)


def _combine_kernel(pos_ref, pos_next_ref, x_ref, meta_ref, fg_ref, ys_ref, out_ref, bufs, sems, *,
                    final_norm):
    tm = x_ref.shape[0]
    i = pl.program_id(0)
    n = pl.num_programs(0)
    slot = lax.rem(i, 2)
    tile_rows = tm * SUBLANES

    def fetch(p_ref, s):
        def issue(c, _):
            for u in range(ROWS_PER_ISSUE):
                r = c * ROWS_PER_ISSUE + u
                for k in range(TOP_K):
                    src = p_ref[0, 0, r * TOP_K + k]
                    pltpu.make_async_copy(_row_view(ys_ref, src), _row_view(bufs.at[s, k], r),
                                          sems.at[s]).start(priority=k)
            return 0
        lax.fori_loop(0, tm // ROWS_PER_ISSUE, issue, 0)

    @pl.when(i == 0)
    def _():
        fetch(pos_ref, 0)

    @pl.when(i + 1 < n)
    def _():
        fetch(pos_next_ref, 1 - slot)

    for k in range(TOP_K):
        pltpu.make_async_copy(ys_ref.at[pl.ds(0, tile_rows), :], bufs.at[slot, k], sems.at[slot]).wait()

    meta = meta_ref[...]
    g0 = meta[:, META_G0:META_G0 + 1]
    g1 = meta[:, META_G1:META_G1 + 1]
    x = x_ref[...]
    cols = []
    for j in range(x.shape[1] // LANES):
        y0 = bufs[slot, 0, pl.ds(j, tm, stride=SUBLANES), :]
        y1 = bufs[slot, 1, pl.ds(j, tm, stride=SUBLANES), :]
        cols.append(x[:, j * LANES:(j + 1) * LANES] + (g0 * y0 + g1 * y1))
    y = jnp.concatenate(cols, axis=1)
    if final_norm:
        y = _rms(y, fg_ref[...])
    out_ref[...] = y


def _combine(x2, meta, pos, ys, final_g, final_norm):
    t, d = x2.shape
    tm = min(GROUP_TILE, t)
    n_tiles = t // tm
    pos3 = pos.reshape(n_tiles, 1, tm * TOP_K)
    grid_spec = pltpu.PrefetchScalarGridSpec(
        num_scalar_prefetch=0,
        grid=(n_tiles,),
        in_specs=[pl.BlockSpec((1, 1, tm * TOP_K), lambda i: (i, 0, 0), memory_space=pltpu.SMEM),
                  pl.BlockSpec((1, 1, tm * TOP_K), lambda i: (jnp.minimum(i + 1, n_tiles - 1), 0, 0),
                               memory_space=pltpu.SMEM),
                  pl.BlockSpec((tm, d), lambda i: (i, 0)),
                  pl.BlockSpec((tm, SUBLANES), lambda i: (i, 0)),
                  _const_spec((1, d)),
                  pl.BlockSpec(memory_space=pl.ANY)],
        out_specs=pl.BlockSpec((tm, d), lambda i: (i, 0)),
        scratch_shapes=[pltpu.VMEM((2, TOP_K, tm * SUBLANES, LANES), F32),
                        pltpu.SemaphoreType.DMA((2,))],
    )
    return pl.pallas_call(
        functools.partial(_combine_kernel, final_norm=final_norm),
        grid_spec=grid_spec,
        out_shape=jax.ShapeDtypeStruct((t, d), F32),
        compiler_params=_cparams(1),
        name="moe_combine",
    )(pos3, pos3, x2, meta, final_g.reshape(1, d), ys)


def _moe_ffn(x2, g, wr, wg, wu, wd, final_g, final_norm):
    t, d = x2.shape
    meta_t, counts = _router(x2, g, wr.T)
    meta = meta_t.T

    cnt = counts[:, 0].astype(jnp.int32)
    padded = ((cnt + GROUP_TILE - 1) // GROUP_TILE) * GROUP_TILE
    ends = jnp.cumsum(padded)
    offs = ends - padded
    n_tiles = (t * TOP_K) // GROUP_TILE + N_EXPERTS
    n_rows = n_tiles * GROUP_TILE
    eid = meta[:, META_E0:META_E1 + 1].astype(jnp.int32)
    rank = meta[:, META_R0:META_R1 + 1].astype(jnp.int32)
    pos = (offs[eid] + rank).reshape(-1)
    tail_tiles = ends[-1] + jnp.arange(N_EXPERTS, dtype=jnp.int32) * GROUP_TILE
    pad_start = jnp.concatenate([ends - GROUP_TILE, tail_tiles])
    pad_on = jnp.concatenate([cnt > 0, tail_tiles < n_rows]).astype(jnp.int32)
    pad_start = jnp.clip(pad_start, 0, n_rows - GROUP_TILE).astype(jnp.int32)
    tile_start = jnp.arange(n_tiles, dtype=jnp.int32) * GROUP_TILE
    last_tile = jnp.maximum(ends[-1] // GROUP_TILE - 1, 0).astype(jnp.int32).reshape(1)
    tile_expert = jnp.sum(tile_start[:, None] >= ends[None, :], axis=1).astype(jnp.int32)
    tile_expert = jnp.minimum(tile_expert, tile_expert[last_tile[0]])

    xs = _dispatch(x2, g, pos, pad_start, pad_on, n_rows)
    ys = _group_ffn(xs, tile_expert, last_tile, wg, wu, wd)
    return _combine(x2, meta, pos, ys, final_g, final_norm)


def kernel(x, attn_norm_g, w_in, pool_w, pool_scale, conv_w, conv_b, conv_ln_g, conv_ln_b,
           group_norm_g, w_out, ffn_norm_g, dense_w_gate, dense_w_up, dense_w_down,
           router_w, moe_w_gate, moe_w_up, moe_w_down, final_norm_g):
    b, s, d = x.shape
    depth = w_in.shape[0]
    t = b * s
    for l in range(depth):
        x2 = x.reshape(t, d)
        qkv, rest = _inproj(x2, attn_norm_g[l], w_in[l].astype(BF16))
        is_moe = l % 2 == 1
        moe_f32 = (moe_w_gate, moe_w_up, moe_w_down) if is_moe else ()
        o_a, moe_bf16 = _attention(qkv.reshape(b, s, -1), moe_f32, l // 2)
        wp_bd = jax.scipy.linalg.block_diag(*[pool_w[l, gi] for gi in range(pool_w.shape[1])]).astype(BF16)
        x = _mixout(x, rest.reshape(b, s, -1), o_a, wp_bd, pool_scale[l], conv_w[l], conv_b[l],
                    conv_ln_g[l], conv_ln_b[l], group_norm_g[l], w_out[l].astype(BF16))
        x2 = x.reshape(t, d)
        last = l == depth - 1
        i = l // 2
        if l % 2 == 0:
            x2 = _dense_ffn(x2, ffn_norm_g[l], dense_w_gate[i].astype(BF16), dense_w_up[i].astype(BF16),
                            dense_w_down[i].astype(BF16), final_norm_g, last)
        else:
            x2 = _moe_ffn(x2, ffn_norm_g[l], router_w[i], *moe_bf16, final_norm_g, last)
        x = x2.reshape(b, s, d)
    return x
```

```python
import functools
import math

import jax
import jax.numpy as jnp
from jax import lax
from jax.experimental import pallas as pl
from jax.experimental.pallas import tpu as pltpu

F32 = jnp.float32
BF16 = jnp.bfloat16

LANES = 128
SUBLANES = 8

N_HEADS = 8
HEAD_DIM = 64
ATTN_W = N_HEADS * HEAD_DIM
POOL_W = 256
CONV_W = 256
POOL_WINDOWS = (2, 4, 8, 16)
CONV_K = 31
HALO = 32
N_EXPERTS = 8
TOP_K = 2
RMS_EPS = 1e-6
LN_EPS = 1e-5

ATTN_BLOCK = 256
EXP_ZERO_BELOW = -105.0

TOKEN_TILE = 512
ROW_CHUNK = 128
GROUP_TILE = 256
FF_CHUNKS = (1024, 1024, 768)

VMEM_LIMIT = 56 * 1024 * 1024


def _cparams(n_axes, vmem=VMEM_LIMIT):
    return pltpu.CompilerParams(dimension_semantics=("arbitrary",) * n_axes, vmem_limit_bytes=vmem)


def _rms(x, g):
    return x * lax.rsqrt(jnp.mean(x * x, axis=-1, keepdims=True) + RMS_EPS) * g


def _dot(a, b):
    return jnp.dot(a, b, preferred_element_type=F32)


def _const_spec(shape):
    zeros = (0,) * len(shape)
    return pl.BlockSpec(shape, lambda *_: zeros, pipeline_mode=pl.Buffered(1))


def _inproj_kernel(x_ref, g_ref, w_ref, qkv_ref, rest_ref):
    h = _rms(x_ref[...], g_ref[...]).astype(BF16)
    nq = qkv_ref.shape[-1]
    qkv_ref[...] = _dot(h, w_ref[:, :nq]).astype(BF16)
    rest_ref[...] = _dot(h, w_ref[:, nq:])


def _inproj(x2, g, w_bf16):
    t, d = x2.shape
    n = w_bf16.shape[1]
    nq = 3 * ATTN_W
    tm = min(TOKEN_TILE, t)
    return pl.pallas_call(
        _inproj_kernel,
        grid=(t // tm,),
        in_specs=[pl.BlockSpec((tm, d), lambda i: (i, 0)),
                  _const_spec((1, d)),
                  _const_spec((d, n))],
        out_specs=[pl.BlockSpec((tm, nq), lambda i: (i, 0)),
                   pl.BlockSpec((tm, n - nq), lambda i: (i, 0))],
        out_shape=[jax.ShapeDtypeStruct((t, nq), BF16),
                   jax.ShapeDtypeStruct((t, n - nq), F32)],
        compiler_params=_cparams(1),
        name="inproj",
    )(x2, g.reshape(1, d), w_bf16)


def _attn_kernel(q_ref, k_ref, v_ref, o_ref, carry_ref):
    blk = q_ref.shape[1]
    n_pairs = q_ref.shape[2] // LANES
    qi = pl.program_id(1)
    scale = jnp.asarray(1.0 / math.sqrt(HEAD_DIM), BF16)

    lane = lax.broadcasted_iota(jnp.int32, (1, LANES), 1)
    head_lanes = (lane < HEAD_DIM, lane >= HEAD_DIM)
    zero_bf = jnp.zeros((), BF16)

    jj = lax.broadcasted_iota(jnp.int32, (2 * blk, blk), 0)
    ss = lax.broadcasted_iota(jnp.int32, (2 * blk, blk), 1)
    upper2 = jnp.where(jnp.where(jj >= blk, jj - blk, jj) > ss, -1.0, 0.0).astype(BF16)

    n_heads = 2 * n_pairs

    def add_blocks(blocks):
        starts = [pl.multiple_of(j * blk, blk) for j, _ in blocks]
        if any(diagonal for _, diagonal in blocks):
            r_idx = lax.broadcasted_iota(jnp.int32, (blk, blk), 0)
            c_idx = lax.broadcasted_iota(jnp.int32, (blk, blk), 1)
            causal = c_idx < r_idx
        n_chains = n_heads * len(blocks)

        def cols(c):
            n = c % n_heads
            return slice((n // 2) * LANES, (n // 2 + 1) * LANES)

        def scores(c):
            q = q_ref[0, :, cols(c)] * scale
            k = k_ref[0, pl.ds(starts[c // n_heads], blk), cols(c)]
            qh = jnp.where(head_lanes[c % 2], q, zero_bf)
            return lax.dot_general(qh, k, (((1,), (1,)), ((), ())), preferred_element_type=F32)

        def log_terms(c, z):
            softplus = jnp.maximum(z, 0.0) + jnp.log(1.0 + jnp.exp(jnp.minimum(z, -z)))
            if blocks[c // n_heads][1]:
                softplus = jnp.where(causal, softplus, 0.0)
            hi = softplus.astype(BF16)
            lo = (softplus - hi.astype(F32)).astype(BF16)
            return z - softplus, jnp.concatenate([hi, lo], axis=1), softplus[:, 0:1]

        def weighted_values(c, log_beta, rest, first_col):
            n, first = c % n_heads, c < n_heads and blocks[0][1]
            arg = log_beta + rest
            if not first:
                arg = arg + carry_ref[n]
            w = jnp.exp(arg)
            if blocks[c // n_heads][1]:
                w = jnp.where(causal, w, 0.0)
            v = v_ref[0, pl.ds(starts[c // n_heads], blk), cols(c)]
            vh = jnp.where(head_lanes[c % 2], v, zero_bf)
            block_sum = rest[:, 0:1] - first_col
            carry = block_sum if first else carry_ref[n] + block_sum
            carry_ref[n] = carry
            return _dot(w.astype(BF16), vh), carry

        z, terms, rest, pv, bound = {}, {}, {}, {}, None
        for step in range(n_chains + 3):
            c4, c3, c2, c1 = step - 3, step - 2, step - 1, step
            if 0 <= c3 < n_chains:
                rest[c3] = _dot(terms[c3][1], upper2)
            if 0 <= c1 < n_chains:
                z[c1] = scores(c1)
            if 0 <= c4 < n_chains:
                pv[c4], carry = weighted_values(c4, terms[c4][0], rest.pop(c4), terms[c4][2])
                del terms[c4]
                if c4 >= n_chains - n_heads:
                    bound = carry if bound is None else jnp.maximum(bound, carry)
                if c4 % 2 == 1:
                    both = pv.pop(c4 - 1) + pv.pop(c4)
                    first = c4 < n_heads and blocks[0][1]
                    o_ref[0, :, cols(c4)] = both if first else o_ref[0, :, cols(c4)] + both
            if 0 <= c2 < n_chains:
                terms[c2] = log_terms(c2, z.pop(c2))
        return (jnp.max(bound) > EXP_ZERO_BELOW).astype(jnp.int32)

    @pl.when(qi == 0)
    def _():
        add_blocks([(qi, True)])

    @pl.when(qi > 0)
    def _():
        go = add_blocks([(qi, True), (qi - 1, False)])

        def cond(state):
            return jnp.logical_and(state[0] >= 0, state[1] > 0)

        def body(state):
            return state[0] - 1, add_blocks([(state[0], False)])

        lax.while_loop(cond, body, (qi - 2, go))


def _attn_and_cast_kernel(*refs, n_cast):
    q_ref, k_ref, v_ref = refs[:3]
    srcs = refs[3:3 + n_cast]
    o_ref = refs[3 + n_cast]
    dsts = refs[4 + n_cast:4 + 2 * n_cast]
    carry_ref = refs[4 + 2 * n_cast]
    for src, dst in zip(srcs, dsts):
        dst[...] = src[...].astype(BF16)
    _attn_kernel(q_ref, k_ref, v_ref, o_ref, carry_ref)


BF16_ROWS = 16


def _attention(qkv3, cast_weights=()):
    b, s, _ = qkv3.shape
    blk = min(ATTN_BLOCK, s)
    n_q = s // blk
    steps = b * n_q
    cast_2d, in_cast, out_cast, out_shapes = [], [], [], []
    for w, index in cast_weights:
        cols = w.shape[-1]
        layer_rows = math.prod(w.shape[1:-1])
        hold = 1
        while steps % hold or layer_rows % (steps // hold) or (layer_rows // (steps // hold)) % BF16_ROWS:
            hold += 1
        n_blocks = steps // hold
        rows = layer_rows // n_blocks

        def in_map(bi, qi, index=index, hold=hold, n_blocks=n_blocks):
            return index * n_blocks + (bi * n_q + qi) // hold, 0

        def out_map(bi, qi, hold=hold):
            return (bi * n_q + qi) // hold, 0

        cast_2d.append(w.reshape(-1, cols))
        in_cast.append(pl.BlockSpec((rows, cols), in_map))
        out_cast.append(pl.BlockSpec((rows, cols), out_map))
        out_shapes.append(jax.ShapeDtypeStruct((layer_rows, cols), BF16))
    outs = pl.pallas_call(
        functools.partial(_attn_and_cast_kernel, n_cast=len(cast_2d)),
        grid=(b, n_q),
        in_specs=[pl.BlockSpec((1, blk, ATTN_W), lambda bi, qi: (bi, qi, 0)),
                  pl.BlockSpec((1, s, ATTN_W), lambda bi, qi: (bi, 0, 1)),
                  pl.BlockSpec((1, s, ATTN_W), lambda bi, qi: (bi, 0, 2))] + in_cast,
        out_specs=[pl.BlockSpec((1, blk, ATTN_W), lambda bi, qi: (bi, qi, 0))] + out_cast,
        out_shape=[jax.ShapeDtypeStruct((b, s, ATTN_W), F32)] + out_shapes,
        scratch_shapes=[pltpu.VMEM((N_HEADS, blk, 1), F32)],
        compiler_params=_cparams(2),
        name="sb_attention",
    )(qkv3, qkv3, qkv3, *cast_2d)
    return outs[0], [o.reshape(w.shape[1:]) for o, (w, _) in zip(outs[1:], cast_weights)]


def _mixout_kernel(x_ref, rest_ref, halo_ref, oa_ref, wp_ref, ps_ref, cw_ref, cb_ref,
                   lg_ref, lb_ref, gn_ref, wo_ref, out_ref, pool_ext, conv_ext, conv_shift):
    tm = x_ref.shape[1]
    ti = pl.program_id(1)
    has_history = (ti > 0).astype(F32)

    def glu(val, gate):
        return val * (1.0 / (1.0 + jnp.exp(-gate)))

    halo = halo_ref[0] * has_history
    cur = rest_ref[0]
    pool_ext[0:HALO, :] = halo[:, 0:POOL_W]
    pool_ext[HALO:HALO + tm, :] = cur[:, 0:POOL_W]
    conv_ext[0:HALO, :] = glu(halo[:, POOL_W:POOL_W + CONV_W], halo[:, POOL_W + CONV_W:])
    conv_ext[HALO:HALO + tm, :] = glu(cur[:, POOL_W:POOL_W + CONV_W], cur[:, POOL_W + CONV_W:])

    lane = lax.broadcasted_iota(jnp.int32, (1, LANES), 1)
    low_half = lane < (LANES // 2)
    rc = min(ROW_CHUNK, tm)
    pooled_rows = []
    for r0 in range(0, tm, rc):
        pos = ti * tm + r0 + lax.broadcasted_iota(jnp.int32, (rc, 1), 0)
        halves = []
        for c, (w_lo, w_hi) in enumerate(((POOL_WINDOWS[0], POOL_WINDOWS[1]),
                                          (POOL_WINDOWS[2], POOL_WINDOWS[3]))):
            cols = slice(c * LANES, (c + 1) * LANES)
            base = HALO + r0
            ident = pool_ext[base:base + rc, cols]
            s_lo = ident
            for j in range(1, w_lo):
                s_lo = s_lo + pool_ext[base - j:base - j + rc, cols]
            s_hi = s_lo
            for j in range(w_lo, w_hi):
                s_hi = s_hi + pool_ext[base - j:base - j + rc, cols]
            total = jnp.where(low_half, s_lo, s_hi)
            win = jnp.where(low_half, w_lo, w_hi)
            count = jnp.minimum(pos + 1, win).astype(F32)
            halves.append(total / count - ident)
        pooled_rows.append(jnp.concatenate(halves, axis=1))
    pooled = jnp.concatenate(pooled_rows, axis=0).astype(BF16)
    o_b = _dot(pooled, wp_ref[...]) * ps_ref[...]

    ext_rows = HALO + tm
    conv_ext[ext_rows:ext_rows + SUBLANES, :] = jnp.zeros((SUBLANES, CONV_W), F32)
    for s in range(1, SUBLANES):
        conv_shift[s - 1] = conv_ext[s:s + ext_rows, :]
    conv_rows = []
    for r0 in range(0, tm, rc):
        part = jnp.zeros((rc, CONV_W), F32) + cb_ref[...]
        for j in range(CONV_K):
            off = HALO - (CONV_K - 1) + j + r0
            s, aligned = off % SUBLANES, off - off % SUBLANES
            window = (conv_ext[aligned:aligned + rc, :] if s == 0
                      else conv_shift[s - 1, aligned:aligned + rc, :])
            part = part + cw_ref[j:j + 1, :] * window
        conv_rows.append(part)
    conv = jnp.concatenate(conv_rows, axis=0)
    mu = jnp.mean(conv, axis=-1, keepdims=True)
    cen = conv - mu
    var = jnp.mean(cen * cen, axis=-1, keepdims=True)
    ln = cen * lax.rsqrt(var + LN_EPS) * lg_ref[...] + lb_ref[...]
    o_c = ln * (1.0 / (1.0 + jnp.exp(-ln)))

    gn = gn_ref[...]
    o = jnp.concatenate([
        _rms(oa_ref[0], gn[:, 0:ATTN_W]),
        _rms(o_b, gn[:, ATTN_W:ATTN_W + POOL_W]),
        _rms(o_c, gn[:, ATTN_W + POOL_W:]),
    ], axis=1).astype(BF16)
    out_ref[0] = x_ref[0] + _dot(o, wo_ref[...])


def _mixout(x3, rest3, oa3, wp_bd, pool_scale, conv_w, conv_b, ln_g, ln_b, gn_g, wo_bf16):
    b, s, d = x3.shape
    tm = min(TOKEN_TILE, s)
    hb = tm // HALO
    nr = rest3.shape[-1]
    row = lambda a: a.reshape(1, -1)
    return pl.pallas_call(
        _mixout_kernel,
        grid=(b, s // tm),
        in_specs=[pl.BlockSpec((1, tm, d), lambda bi, ti: (bi, ti, 0)),
                  pl.BlockSpec((1, tm, nr), lambda bi, ti: (bi, ti, 0)),
                  pl.BlockSpec((1, HALO, nr), lambda bi, ti: (bi, jnp.maximum(ti * hb - 1, 0), 0)),
                  pl.BlockSpec((1, tm, ATTN_W), lambda bi, ti: (bi, ti, 0)),
                  _const_spec((POOL_W, POOL_W)),
                  _const_spec((1, POOL_W)),
                  _const_spec((CONV_K, CONV_W)),
                  _const_spec((1, CONV_W)),
                  _const_spec((1, CONV_W)),
                  _const_spec((1, CONV_W)),
                  _const_spec((1, d)),
                  _const_spec((d, d))],
        out_specs=pl.BlockSpec((1, tm, d), lambda bi, ti: (bi, ti, 0)),
        out_shape=jax.ShapeDtypeStruct((b, s, d), F32),
        scratch_shapes=[pltpu.VMEM((HALO + tm, POOL_W), F32),
                        pltpu.VMEM((HALO + tm + SUBLANES, CONV_W), F32),
                        pltpu.VMEM((SUBLANES - 1, HALO + tm, CONV_W), F32)],
        compiler_params=_cparams(2),
        name="mixout",
    )(x3, rest3, rest3, oa3, wp_bd, row(pool_scale), conv_w, row(conv_b), row(ln_g), row(ln_b),
      row(gn_g), wo_bf16)


def _swiglu_rows(h_bf16, wg_ref, wu_ref, wd_ref, acc):
    f0 = 0
    for fc in FF_CHUNKS:
        gate = _dot(h_bf16, wg_ref[:, f0:f0 + fc])
        up = _dot(h_bf16, wu_ref[:, f0:f0 + fc])
        act = (gate * (1.0 / (1.0 + jnp.exp(-gate))) * up).astype(BF16)
        acc = acc + _dot(act, wd_ref[f0:f0 + fc, :])
        f0 += fc
    return acc


def _dense_ffn_kernel(x_ref, g_ref, wg_ref, wu_ref, wd_ref, fg_ref, out_ref, *, final_norm):
    x = x_ref[...]
    h = _rms(x, g_ref[...]).astype(BF16)
    y = _swiglu_rows(h, wg_ref, wu_ref, wd_ref, x)
    if final_norm:
        y = _rms(y, fg_ref[...])
    out_ref[...] = y


def _dense_ffn(x2, g, wg, wu, wd, final_g, final_norm):
    t, d = x2.shape
    f = wg.shape[1]
    assert sum(FF_CHUNKS) == f
    tm = min(TOKEN_TILE, t)
    return pl.pallas_call(
        functools.partial(_dense_ffn_kernel, final_norm=final_norm),
        grid=(t // tm,),
        in_specs=[pl.BlockSpec((tm, d), lambda i: (i, 0)),
                  _const_spec((1, d)),
                  _const_spec((d, f)),
                  _const_spec((d, f)),
                  _const_spec((f, d)),
                  _const_spec((1, d))],
        out_specs=pl.BlockSpec((tm, d), lambda i: (i, 0)),
        out_shape=jax.ShapeDtypeStruct((t, d), F32),
        compiler_params=_cparams(1),
        name="dense_ffn",
    )(x2, g.reshape(1, d), wg, wu, wd, final_g.reshape(1, d))


META_E0, META_E1, META_G0, META_G1, META_R0, META_R1 = range(6)


def _split3(a):
    p0 = a.astype(BF16)
    r1 = a - p0.astype(F32)
    p1 = r1.astype(BF16)
    p2 = (r1 - p1.astype(F32)).astype(BF16)
    return p0, p1, p2


def _router_kernel(x_ref, g_ref, wrt_ref, meta_ref, counts_ref, run_ref):
    tm = x_ref.shape[0]
    i = pl.program_id(0)
    nt_dims = (((1,), (1,)), ((), ()))

    @pl.when(i == 0)
    def _():
        run_ref[...] = jnp.zeros_like(run_ref)

    h = _rms(x_ref[...], g_ref[...])
    hs = _split3(h)
    ws = [p.astype(F32) for p in _split3(wrt_ref[...])]
    logits = jnp.zeros((N_EXPERTS, tm), F32)
    for a in (2, 1, 0):
        stacked = jnp.concatenate(ws[:3 - a], axis=0).astype(BF16)
        part = lax.dot_general(stacked, hs[a], nt_dims, preferred_element_type=F32)
        for b in range(3 - a):
            logits = logits + part[b * N_EXPERTS:(b + 1) * N_EXPERTS, :]

    eidx = lax.broadcasted_iota(jnp.int32, (N_EXPERTS, tm), 0).astype(F32)
    neg = jnp.float32(-jnp.inf)
    v0 = jnp.max(logits, axis=0, keepdims=True)
    e0 = jnp.min(jnp.where(logits == v0, eidx, float(N_EXPERTS)), axis=0, keepdims=True)
    masked = jnp.where(eidx == e0, neg, logits)
    v1 = jnp.max(masked, axis=0, keepdims=True)
    e1 = jnp.min(jnp.where(masked == v1, eidx, float(N_EXPERTS)), axis=0, keepdims=True)
    ex = jnp.exp(v1 - v0)
    g0 = 1.0 / (1.0 + ex)
    g1 = ex / (1.0 + ex)

    sel0 = eidx == e0
    sel1 = eidx == e1
    onehot = jnp.where(jnp.logical_or(sel0, sel1), 1.0, 0.0)
    rr = lax.broadcasted_iota(jnp.int32, (tm, tm), 0)
    cc = lax.broadcasted_iota(jnp.int32, (tm, tm), 1)
    earlier = jnp.where(rr < cc, 1.0, 0.0).astype(BF16)
    run = run_ref[:, 0:1]
    rank_all = _dot(onehot.astype(BF16), earlier) + run
    r0 = jnp.sum(jnp.where(sel0, rank_all, 0.0), axis=0, keepdims=True)
    r1 = jnp.sum(jnp.where(sel1, rank_all, 0.0), axis=0, keepdims=True)
    new_run = run + jnp.sum(onehot, axis=1, keepdims=True)
    run_ref[...] = jnp.broadcast_to(new_run, run_ref.shape)
    counts_ref[...] = jnp.broadcast_to(new_run, counts_ref.shape)

    zero_row = jnp.zeros((1, tm), F32)
    meta_ref[...] = jnp.concatenate([e0, e1, g0, g1, r0, r1, zero_row, zero_row], axis=0)


def _router(x2, g, wr_t):
    t, d = x2.shape
    tm = min(TOKEN_TILE, t)
    return pl.pallas_call(
        _router_kernel,
        grid=(t // tm,),
        in_specs=[pl.BlockSpec((tm, d), lambda i: (i, 0)),
                  _const_spec((1, d)),
                  _const_spec((N_EXPERTS, d))],
        out_specs=[pl.BlockSpec((SUBLANES, tm), lambda i: (0, i)),
                   pl.BlockSpec((N_EXPERTS, LANES), lambda i: (0, 0))],
        out_shape=[jax.ShapeDtypeStruct((SUBLANES, t), F32),
                   jax.ShapeDtypeStruct((N_EXPERTS, LANES), F32)],
        scratch_shapes=[pltpu.VMEM((N_EXPERTS, LANES), F32)],
        compiler_params=_cparams(1),
        name="moe_router",
    )(x2, g.reshape(1, d), wr_t)


ROWS_PER_ISSUE = 8


def _row_view(ref, row):
    return ref.at[pl.ds(pl.multiple_of(row * SUBLANES, SUBLANES), SUBLANES), :]


def _dispatch_kernel(pos_ref, pad_start_ref, pad_on_ref, x_ref, g_ref, xs_ref, rows, zeros, sems, zsem):
    tm = x_ref.shape[0]
    i = pl.program_id(0)
    n = pl.num_programs(0)
    slot = lax.rem(i, 2)
    tile_rows = tm * SUBLANES

    def drain(s):
        for _ in range(TOP_K):
            pltpu.make_async_copy(rows.at[s], xs_ref.at[pl.ds(0, tile_rows), :], sems.at[s]).wait()

    @pl.when(i == 0)
    def _():
        zeros[...] = jnp.zeros_like(zeros)
        zrows = zeros.shape[0]
        for e in range(pad_start_ref.shape[0]):
            @pl.when(pad_on_ref[e] > 0)
            def _():
                start = pl.multiple_of(pad_start_ref[e] * SUBLANES, SUBLANES)
                pltpu.make_async_copy(zeros, xs_ref.at[pl.ds(start, zrows), :], zsem).start()
        for e in range(pad_start_ref.shape[0]):
            @pl.when(pad_on_ref[e] > 0)
            def _():
                pltpu.make_async_copy(zeros, xs_ref.at[pl.ds(0, zrows), :], zsem).wait()

    @pl.when(i >= 2)
    def _():
        drain(slot)

    h = _rms(x_ref[...], g_ref[...])
    buf = rows.at[slot]
    for j in range(h.shape[1] // LANES):
        buf[pl.ds(j, tm, stride=SUBLANES), :] = h[:, j * LANES:(j + 1) * LANES]

    def issue(c, _):
        for u in range(ROWS_PER_ISSUE):
            r = c * ROWS_PER_ISSUE + u
            for k in range(TOP_K):
                dst = pos_ref[0, 0, k * tm + r]
                pltpu.make_async_copy(_row_view(buf, r), _row_view(xs_ref, dst),
                                      sems.at[slot]).start(priority=k)
        return 0

    lax.fori_loop(0, tm // ROWS_PER_ISSUE, issue, 0)

    @pl.when(i == n - 1)
    def _():
        @pl.when(n >= 2)
        def _():
            drain(1 - slot)
        drain(slot)


def _tile_positions(pos_t, tm):
    n_tiles = pos_t.shape[1] // tm
    return pos_t.reshape(TOP_K, n_tiles, tm).transpose(1, 0, 2).reshape(n_tiles, 1, TOP_K * tm)


def _dispatch(x2, g, pos_t, pad_start, pad_on, n_rows):
    t, d = x2.shape
    tm = min(TOKEN_TILE, t)
    n_tiles = t // tm
    pos3 = _tile_positions(pos_t, tm)
    grid_spec = pltpu.PrefetchScalarGridSpec(
        num_scalar_prefetch=0,
        grid=(n_tiles,),
        in_specs=[pl.BlockSpec((1, 1, tm * TOP_K), lambda i: (i, 0, 0), memory_space=pltpu.SMEM),
                  pl.BlockSpec(memory_space=pltpu.SMEM),
                  pl.BlockSpec(memory_space=pltpu.SMEM),
                  pl.BlockSpec((tm, d), lambda i: (i, 0)),
                  _const_spec((1, d))],
        out_specs=pl.BlockSpec(memory_space=pl.ANY),
        scratch_shapes=[pltpu.VMEM((2, tm * SUBLANES, LANES), F32),
                        pltpu.VMEM((GROUP_TILE * SUBLANES, LANES), F32),
                        pltpu.SemaphoreType.DMA((2,)),
                        pltpu.SemaphoreType.DMA(())],
    )
    return pl.pallas_call(
        _dispatch_kernel,
        grid_spec=grid_spec,
        out_shape=jax.ShapeDtypeStruct((n_rows * SUBLANES, LANES), F32),
        compiler_params=_cparams(1),
        name="moe_dispatch",
    )(pos3, pad_start, pad_on, x2, g.reshape(1, d))


def _group_ffn_kernel(te_ref, last_ref, xs_ref, wg_ref, wu_ref, wd_ref, ys_ref):
    i = pl.program_id(0)
    tg = xs_ref.shape[0] // SUBLANES
    d = wg_ref.shape[1]

    @pl.when(i <= last_ref[0])
    def _():
        x = jnp.concatenate([xs_ref[pl.ds(j, tg, stride=SUBLANES), :] for j in range(d // LANES)],
                            axis=1).astype(BF16)
        y = _swiglu_rows(x, wg_ref.at[0], wu_ref.at[0], wd_ref.at[0], jnp.zeros((tg, d), F32))
        for j in range(d // LANES):
            ys_ref[pl.ds(j, tg, stride=SUBLANES), :] = y[:, j * LANES:(j + 1) * LANES]

    @pl.when(i > last_ref[0])
    def _():
        ys_ref[...] = jnp.zeros_like(ys_ref)


def _group_ffn(xs, tile_expert, last_tile, wg, wu, wd):
    n_tiles = tile_expert.shape[0]
    _, d, f = wg.shape
    blk = GROUP_TILE * SUBLANES
    grid_spec = pltpu.PrefetchScalarGridSpec(
        num_scalar_prefetch=2,
        grid=(n_tiles,),
        in_specs=[pl.BlockSpec((blk, LANES), lambda i, te, last: (jnp.minimum(i, last[0]), 0)),
                  pl.BlockSpec((1, d, f), lambda i, te, last: (te[i], 0, 0)),
                  pl.BlockSpec((1, d, f), lambda i, te, last: (te[i], 0, 0)),
                  pl.BlockSpec((1, f, d), lambda i, te, last: (te[i], 0, 0))],
        out_specs=pl.BlockSpec((blk, LANES), lambda i, te, last: (i, 0)),
    )
    return pl.pallas_call(
        _group_ffn_kernel,
        grid_spec=grid_spec,
        out_shape=jax.ShapeDtypeStruct(xs.shape, F32),
        compiler_params=_cparams(1),
        name="moe_group_ffn",
    )(tile_expert, last_tile, xs, wg, wu, wd)


def _combine_kernel(pos_ref, pos_next_ref, x_ref, meta_ref, fg_ref, ys_ref, out_ref, bufs, sems, *,
                    final_norm):
    tm = x_ref.shape[0]
    i = pl.program_id(0)
    n = pl.num_programs(0)
    slot = lax.rem(i, 2)
    tile_rows = tm * SUBLANES

    def fetch(p_ref, s):
        def issue(c, _):
            for u in range(ROWS_PER_ISSUE):
                r = c * ROWS_PER_ISSUE + u
                for k in range(TOP_K):
                    src = p_ref[0, 0, k * tm + r]
                    pltpu.make_async_copy(_row_view(ys_ref, src), _row_view(bufs.at[s, k], r),
                                          sems.at[s]).start(priority=k)
            return 0
        lax.fori_loop(0, tm // ROWS_PER_ISSUE, issue, 0)

    @pl.when(i == 0)
    def _():
        fetch(pos_ref, 0)

    @pl.when(i + 1 < n)
    def _():
        fetch(pos_next_ref, 1 - slot)

    for k in range(TOP_K):
        pltpu.make_async_copy(ys_ref.at[pl.ds(0, tile_rows), :], bufs.at[slot, k], sems.at[slot]).wait()

    meta = meta_ref[...]
    g0 = meta[:, META_G0:META_G0 + 1]
    g1 = meta[:, META_G1:META_G1 + 1]
    x = x_ref[...]
    cols = []
    for j in range(x.shape[1] // LANES):
        y0 = bufs[slot, 0, pl.ds(j, tm, stride=SUBLANES), :]
        y1 = bufs[slot, 1, pl.ds(j, tm, stride=SUBLANES), :]
        cols.append(x[:, j * LANES:(j + 1) * LANES] + (g0 * y0 + g1 * y1))
    y = jnp.concatenate(cols, axis=1)
    if final_norm:
        y = _rms(y, fg_ref[...])
    out_ref[...] = y


def _combine(x2, meta, pos_t, ys, final_g, final_norm):
    t, d = x2.shape
    tm = min(GROUP_TILE, t)
    n_tiles = t // tm
    pos3 = _tile_positions(pos_t, tm)
    grid_spec = pltpu.PrefetchScalarGridSpec(
        num_scalar_prefetch=0,
        grid=(n_tiles,),
        in_specs=[pl.BlockSpec((1, 1, tm * TOP_K), lambda i: (i, 0, 0), memory_space=pltpu.SMEM),
                  pl.BlockSpec((1, 1, tm * TOP_K), lambda i: (jnp.minimum(i + 1, n_tiles - 1), 0, 0),
                               memory_space=pltpu.SMEM),
                  pl.BlockSpec((tm, d), lambda i: (i, 0)),
                  pl.BlockSpec((tm, SUBLANES), lambda i: (i, 0)),
                  _const_spec((1, d)),
                  pl.BlockSpec(memory_space=pl.ANY)],
        out_specs=pl.BlockSpec((tm, d), lambda i: (i, 0)),
        scratch_shapes=[pltpu.VMEM((2, TOP_K, tm * SUBLANES, LANES), F32),
                        pltpu.SemaphoreType.DMA((2,))],
    )
    return pl.pallas_call(
        functools.partial(_combine_kernel, final_norm=final_norm),
        grid_spec=grid_spec,
        out_shape=jax.ShapeDtypeStruct((t, d), F32),
        compiler_params=_cparams(1),
        name="moe_combine",
    )(pos3, pos3, x2, meta, final_g.reshape(1, d), ys)


def _moe_ffn(x2, g, wr, wg, wu, wd, final_g, final_norm):
    t, d = x2.shape
    meta_t, counts = _router(x2, g, wr.T)
    meta = meta_t.T

    cnt = counts[:, 0].astype(jnp.int32)
    padded = ((cnt + GROUP_TILE - 1) // GROUP_TILE) * GROUP_TILE
    ends = jnp.cumsum(padded)
    offs = ends - padded
    n_tiles = (t * TOP_K) // GROUP_TILE + N_EXPERTS
    n_rows = n_tiles * GROUP_TILE
    eid_t = meta_t[META_E0:META_E1 + 1].astype(jnp.int32)
    rank_t = meta_t[META_R0:META_R1 + 1].astype(jnp.int32)
    pos_t = rank_t
    for e in range(N_EXPERTS):
        pos_t = pos_t + jnp.where(eid_t == e, offs[e], 0)
    tail_tiles = ends[-1] + jnp.arange(N_EXPERTS, dtype=jnp.int32) * GROUP_TILE
    pad_start = jnp.concatenate([ends - GROUP_TILE, tail_tiles])
    pad_on = jnp.concatenate([cnt > 0, tail_tiles < n_rows]).astype(jnp.int32)
    pad_start = jnp.clip(pad_start, 0, n_rows - GROUP_TILE).astype(jnp.int32)
    tile_start = jnp.arange(n_tiles, dtype=jnp.int32) * GROUP_TILE
    last_tile = jnp.maximum(ends[-1] // GROUP_TILE - 1, 0).astype(jnp.int32).reshape(1)
    tile_expert = jnp.sum(tile_start[:, None] >= ends[None, :], axis=1).astype(jnp.int32)
    tile_expert = jnp.minimum(tile_expert, tile_expert[last_tile[0]])

    xs = _dispatch(x2, g, pos_t, pad_start, pad_on, n_rows)
    ys = _group_ffn(xs, tile_expert, last_tile, wg, wu, wd)
    return _combine(x2, meta, pos_t, ys, final_g, final_norm)


def kernel(x, attn_norm_g, w_in, pool_w, pool_scale, conv_w, conv_b, conv_ln_g, conv_ln_b,
           group_norm_g, w_out, ffn_norm_g, dense_w_gate, dense_w_up, dense_w_down,
           router_w, moe_w_gate, moe_w_up, moe_w_down, final_norm_g):
    b, s, d = x.shape
    depth = w_in.shape[0]
    t = b * s
    w_in_bf16 = w_in[0].astype(BF16)
    for l in range(depth):
        x2 = x.reshape(t, d)
        qkv, rest = _inproj(x2, attn_norm_g[l], w_in_bf16)
        last = l == depth - 1
        i = l // 2
        ffn_f32 = ((dense_w_gate, dense_w_up, dense_w_down) if l % 2 == 0
                   else (moe_w_gate, moe_w_up, moe_w_down))
        jobs = [(w, i) for w in ffn_f32] + [(w_out, l)] + ([] if last else [(w_in, l + 1)])
        o_a, casts = _attention(qkv.reshape(b, s, -1), jobs)
        ffn_bf16, w_out_bf16 = casts[:3], casts[3]
        if not last:
            w_in_bf16 = casts[4]
        wp_bd = jax.scipy.linalg.block_diag(*[pool_w[l, gi] for gi in range(pool_w.shape[1])]).astype(BF16)
        x = _mixout(x, rest.reshape(b, s, -1), o_a, wp_bd, pool_scale[l], conv_w[l], conv_b[l],
                    conv_ln_g[l], conv_ln_b[l], group_norm_g[l], w_out_bf16)
        x2 = x.reshape(t, d)
        if l % 2 == 0:
            x2 = _dense_ffn(x2, ffn_norm_g[l], *ffn_bf16, final_norm_g, last)
        else:
            x2 = _moe_ffn(x2, ffn_norm_g[l], router_w[i], *ffn_bf16, final_norm_g, last)
        x = x2.reshape(b, s, d)
    return x
```

```python
import functools
import math

import jax
import jax.numpy as jnp
from jax import lax
from jax.experimental import pallas as pl
from jax.experimental.pallas import tpu as pltpu

F32 = jnp.float32
BF16 = jnp.bfloat16

LANES = 128
SUBLANES = 8

N_HEADS = 8
HEAD_DIM = 64
ATTN_W = N_HEADS * HEAD_DIM
POOL_W = 256
CONV_W = 256
POOL_WINDOWS = (2, 4, 8, 16)
CONV_K = 31
HALO = 32
N_EXPERTS = 8
TOP_K = 2
RMS_EPS = 1e-6
LN_EPS = 1e-5

ATTN_BLOCK = 256
EXP_ZERO_BELOW = -105.0

TOKEN_TILE = 512
ROW_CHUNK = 128
GROUP_TILE = 256
FF_CHUNKS = (1024, 1024, 768)
QKV_CHUNK = 256

VMEM_LIMIT = 56 * 1024 * 1024


def _cparams(n_axes, vmem=VMEM_LIMIT):
    return pltpu.CompilerParams(dimension_semantics=("arbitrary",) * n_axes, vmem_limit_bytes=vmem)


def _rms(x, g):
    return x * lax.rsqrt(jnp.mean(x * x, axis=-1, keepdims=True) + RMS_EPS) * g


def _dot(a, b):
    return jnp.dot(a, b, preferred_element_type=F32)


def _const_spec(shape):
    zeros = (0,) * len(shape)
    return pl.BlockSpec(shape, lambda *_: zeros, pipeline_mode=pl.Buffered(1))


def _sigmoid(a):
    return 1.0 / (1.0 + jnp.exp(-a))


def _inproj_mix_kernel(x_ref, g_ref, w_ref, wp_ref, ps_ref, cw_ref, cb_ref, lg_ref, lb_ref, gn_ref,
                       qkv_ref, obc_ref, pool_ext, conv_ext, conv_shift, *, tiles_per_seq):
    tm = x_ref.shape[0]
    i = pl.program_id(0)
    in_sequence = lax.rem(i, tiles_per_seq) > 0
    ext_rows = HALO + tm

    @pl.when(i == 0)
    def _():
        pool_ext[...] = jnp.zeros_like(pool_ext)
        conv_ext[...] = jnp.zeros_like(conv_ext)

    h = _rms(x_ref[...], g_ref[...]).astype(BF16)
    nq = qkv_ref.shape[-1]
    rest = _dot(h, w_ref[:, nq:])
    pool_ext[0:HALO, :] = jnp.where(in_sequence, pool_ext[tm:ext_rows, :], 0.0)
    conv_ext[0:HALO, :] = jnp.where(in_sequence, conv_ext[tm:ext_rows, :], 0.0)
    pool_ext[HALO:ext_rows, :] = rest[:, 0:POOL_W]
    conv_ext[HALO:ext_rows, :] = (rest[:, POOL_W:POOL_W + CONV_W]
                                  * _sigmoid(rest[:, POOL_W + CONV_W:]))

    qkv_chunks = iter(range(nq // QKV_CHUNK))

    def qkv_part():
        c = next(qkv_chunks, None)
        if c is not None:
            cols = slice(c * QKV_CHUNK, (c + 1) * QKV_CHUNK)
            qkv_ref[:, cols] = _dot(h, w_ref[:, cols]).astype(BF16)

    lane = lax.broadcasted_iota(jnp.int32, (1, LANES), 1)
    low_half = lane < (LANES // 2)
    rc = min(ROW_CHUNK, tm)
    seq_pos = lax.rem(i, tiles_per_seq) * tm
    pooled_rows = []
    for r0 in range(0, tm, rc):
        pos = seq_pos + r0 + lax.broadcasted_iota(jnp.int32, (rc, 1), 0)
        halves = []
        for c, (w_lo, w_hi) in enumerate(((POOL_WINDOWS[0], POOL_WINDOWS[1]),
                                          (POOL_WINDOWS[2], POOL_WINDOWS[3]))):
            cols = slice(c * LANES, (c + 1) * LANES)
            base = HALO + r0
            ident = pool_ext[base:base + rc, cols]
            s_lo = ident
            for j in range(1, w_lo):
                s_lo = s_lo + pool_ext[base - j:base - j + rc, cols]
            s_hi = s_lo
            for j in range(w_lo, w_hi):
                s_hi = s_hi + pool_ext[base - j:base - j + rc, cols]
            total = jnp.where(low_half, s_lo, s_hi)
            win = jnp.where(low_half, w_lo, w_hi)
            count = jnp.minimum(pos + 1, win).astype(F32)
            halves.append(total / count - ident)
        pooled_rows.append(jnp.concatenate(halves, axis=1))
        if r0 % (2 * rc) == 0:
            qkv_part()
    pooled = jnp.concatenate(pooled_rows, axis=0).astype(BF16)
    o_b = _dot(pooled, wp_ref[...]) * ps_ref[...]

    for s in range(1, SUBLANES):
        conv_shift[s - 1] = conv_ext[s:s + ext_rows, :]
    conv_rows = []
    for r0 in range(0, tm, rc):
        qkv_part()
        part = jnp.zeros((rc, CONV_W), F32) + cb_ref[...]
        for j in range(CONV_K):
            off = HALO - (CONV_K - 1) + j + r0
            s, aligned = off % SUBLANES, off - off % SUBLANES
            window = (conv_ext[aligned:aligned + rc, :] if s == 0
                      else conv_shift[s - 1, aligned:aligned + rc, :])
            part = part + cw_ref[j:j + 1, :] * window
        conv_rows.append(part)
    for _ in qkv_chunks:
        raise AssertionError("tile too small to place every q/k/v matmul chunk")
    conv = jnp.concatenate(conv_rows, axis=0)
    mu = jnp.mean(conv, axis=-1, keepdims=True)
    cen = conv - mu
    var = jnp.mean(cen * cen, axis=-1, keepdims=True)
    ln = cen * lax.rsqrt(var + LN_EPS) * lg_ref[...] + lb_ref[...]
    o_c = ln * _sigmoid(ln)

    gn = gn_ref[...]
    obc_ref[...] = jnp.concatenate([_rms(o_b, gn[:, ATTN_W:ATTN_W + POOL_W]),
                                    _rms(o_c, gn[:, ATTN_W + POOL_W:])], axis=1).astype(BF16)


def _inproj_mix(x2, seq_len, g, w_bf16, wp_bd, pool_scale, conv_w, conv_b, ln_g, ln_b, gn_g):
    t, d = x2.shape
    n = w_bf16.shape[1]
    nq = 3 * ATTN_W
    tm = min(TOKEN_TILE, seq_len)
    assert HALO >= CONV_K - 1 and HALO >= max(POOL_WINDOWS) - 1 and tm >= HALO
    row = lambda a: a.reshape(1, -1)
    return pl.pallas_call(
        functools.partial(_inproj_mix_kernel, tiles_per_seq=seq_len // tm),
        grid=(t // tm,),
        in_specs=[pl.BlockSpec((tm, d), lambda i: (i, 0)),
                  _const_spec((1, d)),
                  _const_spec((d, n)),
                  _const_spec((POOL_W, POOL_W)),
                  _const_spec((1, POOL_W)),
                  _const_spec((CONV_K, CONV_W)),
                  _const_spec((1, CONV_W)),
                  _const_spec((1, CONV_W)),
                  _const_spec((1, CONV_W)),
                  _const_spec((1, d))],
        out_specs=[pl.BlockSpec((tm, nq), lambda i: (i, 0)),
                   pl.BlockSpec((tm, POOL_W + CONV_W), lambda i: (i, 0))],
        out_shape=[jax.ShapeDtypeStruct((t, nq), BF16),
                   jax.ShapeDtypeStruct((t, POOL_W + CONV_W), BF16)],
        scratch_shapes=[pltpu.VMEM((HALO + tm, POOL_W), F32),
                        pltpu.VMEM((HALO + tm + SUBLANES, CONV_W), F32),
                        pltpu.VMEM((SUBLANES - 1, HALO + tm, CONV_W), F32)],
        compiler_params=_cparams(1),
        name="inproj_mix",
    )(x2, row(g), w_bf16, wp_bd, row(pool_scale), conv_w, row(conv_b), row(ln_g), row(ln_b), row(gn_g))


def _attn_kernel(q_ref, k_ref, v_ref, o_ref, carry_ref):
    blk = q_ref.shape[1]
    n_pairs = q_ref.shape[2] // LANES
    qi = pl.program_id(1)
    scale = jnp.asarray(1.0 / math.sqrt(HEAD_DIM), BF16)

    lane = lax.broadcasted_iota(jnp.int32, (1, LANES), 1)
    head_lanes = (lane < HEAD_DIM, lane >= HEAD_DIM)
    zero_bf = jnp.zeros((), BF16)

    jj = lax.broadcasted_iota(jnp.int32, (blk, blk), 0)
    ss = lax.broadcasted_iota(jnp.int32, (blk, blk), 1)
    upper2 = jnp.where(jj > ss, -1.0, 0.0).astype(BF16)

    n_heads = 2 * n_pairs

    def add_blocks(blocks):
        starts = [pl.multiple_of(j * blk, blk) for j, _ in blocks]
        if any(diagonal for _, diagonal in blocks):
            r_idx = lax.broadcasted_iota(jnp.int32, (blk, blk), 0)
            c_idx = lax.broadcasted_iota(jnp.int32, (blk, blk), 1)
            causal = c_idx < r_idx
        n_chains = n_heads * len(blocks)

        def cols(c):
            n = c % n_heads
            return slice((n // 2) * LANES, (n // 2 + 1) * LANES)

        def scores(c):
            q = q_ref[0, :, cols(c)] * scale
            k = k_ref[0, pl.ds(starts[c // n_heads], blk), cols(c)]
            qh = jnp.where(head_lanes[c % 2], q, zero_bf)
            return lax.dot_general(qh, k, (((1,), (1,)), ((), ())), preferred_element_type=F32)

        def log_terms(c, z):
            softplus = jnp.maximum(z, 0.0) + jnp.log(1.0 + jnp.exp(jnp.minimum(z, -z)))
            if blocks[c // n_heads][1]:
                softplus = jnp.where(causal, softplus, 0.0)
            return z - softplus, softplus.astype(BF16), softplus[:, 0:1]

        def weighted_values(c, log_beta, rest, first_col):
            n, first = c % n_heads, c < n_heads and blocks[0][1]
            arg = log_beta + rest
            if not first:
                arg = arg + carry_ref[n]
            w = jnp.exp(arg)
            if blocks[c // n_heads][1]:
                w = jnp.where(causal, w, 0.0)
            v = v_ref[0, pl.ds(starts[c // n_heads], blk), cols(c)]
            vh = jnp.where(head_lanes[c % 2], v, zero_bf)
            block_sum = rest[:, 0:1] - first_col
            carry = block_sum if first else carry_ref[n] + block_sum
            carry_ref[n] = carry
            return _dot(w.astype(BF16), vh), carry

        z, terms, rest, pv, bound = {}, {}, {}, {}, None
        for step in range(n_chains + 3):
            c4, c3, c2, c1 = step - 3, step - 2, step - 1, step
            if 0 <= c3 < n_chains:
                rest[c3] = _dot(terms[c3][1], upper2)
            if 0 <= c1 < n_chains:
                z[c1] = scores(c1)
            if 0 <= c4 < n_chains:
                pv[c4], carry = weighted_values(c4, terms[c4][0], rest.pop(c4), terms[c4][2])
                del terms[c4]
                if c4 >= n_chains - n_heads:
                    bound = carry if bound is None else jnp.maximum(bound, carry)
                if c4 % 2 == 1:
                    both = pv.pop(c4 - 1) + pv.pop(c4)
                    first = c4 < n_heads and blocks[0][1]
                    o_ref[0, :, cols(c4)] = both if first else o_ref[0, :, cols(c4)] + both
            if 0 <= c2 < n_chains:
                terms[c2] = log_terms(c2, z.pop(c2))
        return (jnp.max(bound) > EXP_ZERO_BELOW).astype(jnp.int32)

    @pl.when(qi == 0)
    def _():
        add_blocks([(qi, True)])

    @pl.when(qi > 0)
    def _():
        go = add_blocks([(qi, True), (qi - 1, False)])

        def cond(state):
            return jnp.logical_and(state[0] >= 0, state[1] > 0)

        def body(state):
            return state[0] - 1, add_blocks([(state[0], False)])

        lax.while_loop(cond, body, (qi - 2, go))


def _attn_and_cast_kernel(*refs, n_cast):
    q_ref, k_ref, v_ref = refs[:3]
    srcs = refs[3:3 + n_cast]
    o_ref = refs[3 + n_cast]
    dsts = refs[4 + n_cast:4 + 2 * n_cast]
    carry_ref = refs[4 + 2 * n_cast]
    for src, dst in zip(srcs, dsts):
        dst[...] = src[...].astype(BF16)
    _attn_kernel(q_ref, k_ref, v_ref, o_ref, carry_ref)


BF16_ROWS = 16


def _attention(qkv3, cast_weights=()):
    b, s, _ = qkv3.shape
    blk = min(ATTN_BLOCK, s)
    n_q = s // blk
    steps = b * n_q
    cast_2d, in_cast, out_cast, out_shapes = [], [], [], []
    for w, index in cast_weights:
        cols = w.shape[-1]
        layer_rows = math.prod(w.shape[1:-1])
        hold = 1
        while steps % hold or layer_rows % (steps // hold) or (layer_rows // (steps // hold)) % BF16_ROWS:
            hold += 1
        n_blocks = steps // hold
        rows = layer_rows // n_blocks

        def in_map(bi, qi, index=index, hold=hold, n_blocks=n_blocks):
            return index * n_blocks + (bi * n_q + qi) // hold, 0

        def out_map(bi, qi, hold=hold):
            return (bi * n_q + qi) // hold, 0

        cast_2d.append(w.reshape(-1, cols))
        in_cast.append(pl.BlockSpec((rows, cols), in_map))
        out_cast.append(pl.BlockSpec((rows, cols), out_map))
        out_shapes.append(jax.ShapeDtypeStruct((layer_rows, cols), BF16))
    outs = pl.pallas_call(
        functools.partial(_attn_and_cast_kernel, n_cast=len(cast_2d)),
        grid=(b, n_q),
        in_specs=[pl.BlockSpec((1, blk, ATTN_W), lambda bi, qi: (bi, qi, 0)),
                  pl.BlockSpec((1, s, ATTN_W), lambda bi, qi: (bi, 0, 1)),
                  pl.BlockSpec((1, s, ATTN_W), lambda bi, qi: (bi, 0, 2))] + in_cast,
        out_specs=[pl.BlockSpec((1, blk, ATTN_W), lambda bi, qi: (bi, qi, 0))] + out_cast,
        out_shape=[jax.ShapeDtypeStruct((b, s, ATTN_W), F32)] + out_shapes,
        scratch_shapes=[pltpu.VMEM((N_HEADS, blk, 1), F32)],
        compiler_params=_cparams(2),
        name="sb_attention",
    )(qkv3, qkv3, qkv3, *cast_2d)
    return outs[0], [o.reshape(w.shape[1:]) for o, (w, _) in zip(outs[1:], cast_weights)]


def _outproj_rows(x_ref, oa_ref, obc_ref, gn_ref, wo_ref):
    oa = _rms(oa_ref[...], gn_ref[:, 0:ATTN_W]).astype(BF16)
    o = jnp.concatenate([oa, obc_ref[...]], axis=1)
    return x_ref[...] + _dot(o, wo_ref[...])


def _swiglu_rows(h_bf16, wg_ref, wu_ref, wd_ref, acc):
    f0 = 0
    for fc in FF_CHUNKS:
        gate = _dot(h_bf16, wg_ref[:, f0:f0 + fc])
        up = _dot(h_bf16, wu_ref[:, f0:f0 + fc])
        act = (gate * _sigmoid(gate) * up).astype(BF16)
        acc = acc + _dot(act, wd_ref[f0:f0 + fc, :])
        f0 += fc
    return acc


def _outproj_dense_kernel(x_ref, oa_ref, obc_ref, gn_ref, wo_ref, g_ref, wg_ref, wu_ref, wd_ref,
                          fg_ref, out_ref, *, final_norm):
    x = _outproj_rows(x_ref, oa_ref, obc_ref, gn_ref, wo_ref)
    h = _rms(x, g_ref[...]).astype(BF16)
    y = _swiglu_rows(h, wg_ref, wu_ref, wd_ref, x)
    if final_norm:
        y = _rms(y, fg_ref[...])
    out_ref[...] = y


def _outproj_dense(x2, oa2, obc, gn_g, wo, g, wg, wu, wd, final_g, final_norm):
    t, d = x2.shape
    f = wg.shape[1]
    assert sum(FF_CHUNKS) == f
    tm = min(TOKEN_TILE, t)
    row = lambda a: a.reshape(1, -1)
    return pl.pallas_call(
        functools.partial(_outproj_dense_kernel, final_norm=final_norm),
        grid=(t // tm,),
        in_specs=[pl.BlockSpec((tm, d), lambda i: (i, 0)),
                  pl.BlockSpec((tm, ATTN_W), lambda i: (i, 0)),
                  pl.BlockSpec((tm, POOL_W + CONV_W), lambda i: (i, 0)),
                  _const_spec((1, d)),
                  _const_spec((d, d)),
                  _const_spec((1, d)),
                  _const_spec((d, f)),
                  _const_spec((d, f)),
                  _const_spec((f, d)),
                  _const_spec((1, d))],
        out_specs=pl.BlockSpec((tm, d), lambda i: (i, 0)),
        out_shape=jax.ShapeDtypeStruct((t, d), F32),
        compiler_params=_cparams(1),
        name="outproj_dense_ffn",
    )(x2, oa2, obc, row(gn_g), wo, row(g), wg, wu, wd, row(final_g))


META_E0, META_E1, META_G0, META_G1, META_R0, META_R1 = range(6)


def _split3(a):
    p0 = a.astype(BF16)
    r1 = a - p0.astype(F32)
    p1 = r1.astype(BF16)
    p2 = (r1 - p1.astype(F32)).astype(BF16)
    return p0, p1, p2


def _outproj_router_kernel(x_ref, oa_ref, obc_ref, gn_ref, wo_ref, g_ref, wrt_ref,
                           x1_ref, meta_ref, counts_ref, run_ref):
    tm = x_ref.shape[0]
    i = pl.program_id(0)
    nt_dims = (((1,), (1,)), ((), ()))

    @pl.when(i == 0)
    def _():
        run_ref[...] = jnp.zeros_like(run_ref)

    x1 = _outproj_rows(x_ref, oa_ref, obc_ref, gn_ref, wo_ref)
    x1_ref[...] = x1
    h = _rms(x1, g_ref[...])
    hs = _split3(h)
    ws = [p.astype(F32) for p in _split3(wrt_ref[...])]
    logits = jnp.zeros((N_EXPERTS, tm), F32)
    for a in (2, 1, 0):
        stacked = jnp.concatenate(ws[:3 - a], axis=0).astype(BF16)
        part = lax.dot_general(stacked, hs[a], nt_dims, preferred_element_type=F32)
        for b in range(3 - a):
            logits = logits + part[b * N_EXPERTS:(b + 1) * N_EXPERTS, :]

    eidx = lax.broadcasted_iota(jnp.int32, (N_EXPERTS, tm), 0).astype(F32)
    neg = jnp.float32(-jnp.inf)
    v0 = jnp.max(logits, axis=0, keepdims=True)
    e0 = jnp.min(jnp.where(logits == v0, eidx, float(N_EXPERTS)), axis=0, keepdims=True)
    masked = jnp.where(eidx == e0, neg, logits)
    v1 = jnp.max(masked, axis=0, keepdims=True)
    e1 = jnp.min(jnp.where(masked == v1, eidx, float(N_EXPERTS)), axis=0, keepdims=True)
    ex = jnp.exp(v1 - v0)
    g0 = 1.0 / (1.0 + ex)
    g1 = ex / (1.0 + ex)

    sel0 = eidx == e0
    sel1 = eidx == e1
    onehot = jnp.where(jnp.logical_or(sel0, sel1), 1.0, 0.0)
    rr = lax.broadcasted_iota(jnp.int32, (tm, tm), 0)
    cc = lax.broadcasted_iota(jnp.int32, (tm, tm), 1)
    earlier = jnp.where(rr < cc, 1.0, 0.0).astype(BF16)
    run = run_ref[:, 0:1]
    rank_all = _dot(onehot.astype(BF16), earlier) + run
    r0 = jnp.sum(jnp.where(sel0, rank_all, 0.0), axis=0, keepdims=True)
    r1 = jnp.sum(jnp.where(sel1, rank_all, 0.0), axis=0, keepdims=True)
    new_run = run + jnp.sum(onehot, axis=1, keepdims=True)
    run_ref[...] = jnp.broadcast_to(new_run, run_ref.shape)
    counts_ref[...] = jnp.broadcast_to(new_run, counts_ref.shape)

    zero_row = jnp.zeros((1, tm), F32)
    meta_ref[...] = jnp.concatenate([e0, e1, g0, g1, r0, r1, zero_row, zero_row], axis=0)


def _outproj_router(x2, oa2, obc, gn_g, wo, g, wr_t):
    t, d = x2.shape
    tm = min(TOKEN_TILE, t)
    row = lambda a: a.reshape(1, -1)
    return pl.pallas_call(
        _outproj_router_kernel,
        grid=(t // tm,),
        in_specs=[pl.BlockSpec((tm, d), lambda i: (i, 0)),
                  pl.BlockSpec((tm, ATTN_W), lambda i: (i, 0)),
                  pl.BlockSpec((tm, POOL_W + CONV_W), lambda i: (i, 0)),
                  _const_spec((1, d)),
                  _const_spec((d, d)),
                  _const_spec((1, d)),
                  _const_spec((N_EXPERTS, d))],
        out_specs=[pl.BlockSpec((tm, d), lambda i: (i, 0)),
                   pl.BlockSpec((SUBLANES, tm), lambda i: (0, i)),
                   pl.BlockSpec((N_EXPERTS, LANES), lambda i: (0, 0))],
        out_shape=[jax.ShapeDtypeStruct((t, d), F32),
                   jax.ShapeDtypeStruct((SUBLANES, t), F32),
                   jax.ShapeDtypeStruct((N_EXPERTS, LANES), F32)],
        scratch_shapes=[pltpu.VMEM((N_EXPERTS, LANES), F32)],
        compiler_params=_cparams(1),
        name="outproj_router",
    )(x2, oa2, obc, row(gn_g), wo, row(g), wr_t)


ROWS_PER_ISSUE = 8


def _row_view(ref, row):
    return ref.at[pl.ds(pl.multiple_of(row * SUBLANES, SUBLANES), SUBLANES), :]


def _dispatch_kernel(pos_ref, pad_start_ref, pad_on_ref, x_ref, g_ref, xs_ref, rows, zeros, sems, zsem):
    tm = x_ref.shape[0]
    i = pl.program_id(0)
    n = pl.num_programs(0)
    slot = lax.rem(i, 2)
    tile_rows = tm * SUBLANES

    def drain(s):
        for _ in range(TOP_K):
            pltpu.make_async_copy(rows.at[s], xs_ref.at[pl.ds(0, tile_rows), :], sems.at[s]).wait()

    @pl.when(i == 0)
    def _():
        zeros[...] = jnp.zeros_like(zeros)
        zrows = zeros.shape[0]
        for e in range(pad_start_ref.shape[0]):
            @pl.when(pad_on_ref[e] > 0)
            def _():
                start = pl.multiple_of(pad_start_ref[e] * SUBLANES, SUBLANES)
                pltpu.make_async_copy(zeros, xs_ref.at[pl.ds(start, zrows), :], zsem).start()
        for e in range(pad_start_ref.shape[0]):
            @pl.when(pad_on_ref[e] > 0)
            def _():
                pltpu.make_async_copy(zeros, xs_ref.at[pl.ds(0, zrows), :], zsem).wait()

    @pl.when(i >= 2)
    def _():
        drain(slot)

    h = _rms(x_ref[...], g_ref[...])
    buf = rows.at[slot]
    for j in range(h.shape[1] // LANES):
        buf[pl.ds(j, tm, stride=SUBLANES), :] = h[:, j * LANES:(j + 1) * LANES]

    def issue(c, _):
        for u in range(ROWS_PER_ISSUE):
            r = c * ROWS_PER_ISSUE + u
            for k in range(TOP_K):
                dst = pos_ref[0, 0, k * tm + r]
                pltpu.make_async_copy(_row_view(buf, r), _row_view(xs_ref, dst),
                                      sems.at[slot]).start(priority=k)
        return 0

    lax.fori_loop(0, tm // ROWS_PER_ISSUE, issue, 0)

    @pl.when(i == n - 1)
    def _():
        @pl.when(n >= 2)
        def _():
            drain(1 - slot)
        drain(slot)


def _tile_positions(pos_t, tm):
    n_tiles = pos_t.shape[1] // tm
    return pos_t.reshape(TOP_K, n_tiles, tm).transpose(1, 0, 2).reshape(n_tiles, 1, TOP_K * tm)


def _dispatch(x2, g, pos_t, pad_start, pad_on, n_rows):
    t, d = x2.shape
    tm = min(TOKEN_TILE, t)
    n_tiles = t // tm
    pos3 = _tile_positions(pos_t, tm)
    grid_spec = pltpu.PrefetchScalarGridSpec(
        num_scalar_prefetch=0,
        grid=(n_tiles,),
        in_specs=[pl.BlockSpec((1, 1, tm * TOP_K), lambda i: (i, 0, 0), memory_space=pltpu.SMEM),
                  pl.BlockSpec(memory_space=pltpu.SMEM),
                  pl.BlockSpec(memory_space=pltpu.SMEM),
                  pl.BlockSpec((tm, d), lambda i: (i, 0)),
                  _const_spec((1, d))],
        out_specs=pl.BlockSpec(memory_space=pl.ANY),
        scratch_shapes=[pltpu.VMEM((2, tm * SUBLANES, LANES), F32),
                        pltpu.VMEM((GROUP_TILE * SUBLANES, LANES), F32),
                        pltpu.SemaphoreType.DMA((2,)),
                        pltpu.SemaphoreType.DMA(())],
    )
    return pl.pallas_call(
        _dispatch_kernel,
        grid_spec=grid_spec,
        out_shape=jax.ShapeDtypeStruct((n_rows * SUBLANES, LANES), F32),
        compiler_params=_cparams(1),
        name="moe_dispatch",
    )(pos3, pad_start, pad_on, x2, g.reshape(1, d))


def _group_ffn_kernel(te_ref, last_ref, xs_ref, wg_ref, wu_ref, wd_ref, ys_ref):
    i = pl.program_id(0)
    tg = xs_ref.shape[0] // SUBLANES
    d = wg_ref.shape[1]

    @pl.when(i <= last_ref[0])
    def _():
        x = jnp.concatenate([xs_ref[pl.ds(j, tg, stride=SUBLANES), :] for j in range(d // LANES)],
                            axis=1).astype(BF16)
        y = _swiglu_rows(x, wg_ref.at[0], wu_ref.at[0], wd_ref.at[0], jnp.zeros((tg, d), F32))
        for j in range(d // LANES):
            ys_ref[pl.ds(j, tg, stride=SUBLANES), :] = y[:, j * LANES:(j + 1) * LANES]

    @pl.when(i > last_ref[0])
    def _():
        ys_ref[...] = jnp.zeros_like(ys_ref)


def _group_ffn(xs, tile_expert, last_tile, wg, wu, wd):
    n_tiles = tile_expert.shape[0]
    _, d, f = wg.shape
    blk = GROUP_TILE * SUBLANES
    grid_spec = pltpu.PrefetchScalarGridSpec(
        num_scalar_prefetch=2,
        grid=(n_tiles,),
        in_specs=[pl.BlockSpec((blk, LANES), lambda i, te, last: (jnp.minimum(i, last[0]), 0)),
                  pl.BlockSpec((1, d, f), lambda i, te, last: (te[i], 0, 0)),
                  pl.BlockSpec((1, d, f), lambda i, te, last: (te[i], 0, 0)),
                  pl.BlockSpec((1, f, d), lambda i, te, last: (te[i], 0, 0))],
        out_specs=pl.BlockSpec((blk, LANES), lambda i, te, last: (i, 0)),
    )
    return pl.pallas_call(
        _group_ffn_kernel,
        grid_spec=grid_spec,
        out_shape=jax.ShapeDtypeStruct(xs.shape, F32),
        compiler_params=_cparams(1),
        name="moe_group_ffn",
    )(tile_expert, last_tile, xs, wg, wu, wd)


def _combine_kernel(pos_ref, pos_next_ref, x_ref, meta_ref, fg_ref, ys_ref, out_ref, bufs, sems, *,
                    final_norm):
    tm = x_ref.shape[0]
    i = pl.program_id(0)
    n = pl.num_programs(0)
    slot = lax.rem(i, 2)
    tile_rows = tm * SUBLANES

    def fetch(p_ref, s):
        def issue(c, _):
            for u in range(ROWS_PER_ISSUE):
                r = c * ROWS_PER_ISSUE + u
                for k in range(TOP_K):
                    src = p_ref[0, 0, k * tm + r]
                    pltpu.make_async_copy(_row_view(ys_ref, src), _row_view(bufs.at[s, k], r),
                                          sems.at[s]).start(priority=k)
            return 0
        lax.fori_loop(0, tm // ROWS_PER_ISSUE, issue, 0)

    @pl.when(i == 0)
    def _():
        fetch(pos_ref, 0)

    @pl.when(i + 1 < n)
    def _():
        fetch(pos_next_ref, 1 - slot)

    for k in range(TOP_K):
        pltpu.make_async_copy(ys_ref.at[pl.ds(0, tile_rows), :], bufs.at[slot, k], sems.at[slot]).wait()

    meta = meta_ref[...]
    g0 = meta[:, META_G0:META_G0 + 1]
    g1 = meta[:, META_G1:META_G1 + 1]
    x = x_ref[...]
    cols = []
    for j in range(x.shape[1] // LANES):
        y0 = bufs[slot, 0, pl.ds(j, tm, stride=SUBLANES), :]
        y1 = bufs[slot, 1, pl.ds(j, tm, stride=SUBLANES), :]
        cols.append(x[:, j * LANES:(j + 1) * LANES] + (g0 * y0 + g1 * y1))
    y = jnp.concatenate(cols, axis=1)
    if final_norm:
        y = _rms(y, fg_ref[...])
    out_ref[...] = y


def _combine(x2, meta, pos_t, ys, final_g, final_norm):
    t, d = x2.shape
    tm = min(GROUP_TILE, t)
    n_tiles = t // tm
    pos3 = _tile_positions(pos_t, tm)
    grid_spec = pltpu.PrefetchScalarGridSpec(
        num_scalar_prefetch=0,
        grid=(n_tiles,),
        in_specs=[pl.BlockSpec((1, 1, tm * TOP_K), lambda i: (i, 0, 0), memory_space=pltpu.SMEM),
                  pl.BlockSpec((1, 1, tm * TOP_K), lambda i: (jnp.minimum(i + 1, n_tiles - 1), 0, 0),
                               memory_space=pltpu.SMEM),
                  pl.BlockSpec((tm, d), lambda i: (i, 0)),
                  pl.BlockSpec((tm, SUBLANES), lambda i: (i, 0)),
                  _const_spec((1, d)),
                  pl.BlockSpec(memory_space=pl.ANY)],
        out_specs=pl.BlockSpec((tm, d), lambda i: (i, 0)),
        scratch_shapes=[pltpu.VMEM((2, TOP_K, tm * SUBLANES, LANES), F32),
                        pltpu.SemaphoreType.DMA((2,))],
    )
    return pl.pallas_call(
        functools.partial(_combine_kernel, final_norm=final_norm),
        grid_spec=grid_spec,
        out_shape=jax.ShapeDtypeStruct((t, d), F32),
        compiler_params=_cparams(1),
        name="moe_combine",
    )(pos3, pos3, x2, meta, final_g.reshape(1, d), ys)


def _moe_ffn(x2, meta_t, counts, g, wg, wu, wd, final_g, final_norm):
    t, d = x2.shape
    meta = meta_t.T

    cnt = counts[:, 0].astype(jnp.int32)
    padded = ((cnt + GROUP_TILE - 1) // GROUP_TILE) * GROUP_TILE
    ends = jnp.cumsum(padded)
    offs = ends - padded
    n_tiles = (t * TOP_K) // GROUP_TILE + N_EXPERTS
    n_rows = n_tiles * GROUP_TILE
    eid_t = meta_t[META_E0:META_E1 + 1].astype(jnp.int32)
    rank_t = meta_t[META_R0:META_R1 + 1].astype(jnp.int32)
    pos_t = rank_t
    for e in range(N_EXPERTS):
        pos_t = pos_t + jnp.where(eid_t == e, offs[e], 0)
    tail_tiles = ends[-1] + jnp.arange(N_EXPERTS, dtype=jnp.int32) * GROUP_TILE
    pad_start = jnp.concatenate([ends - GROUP_TILE, tail_tiles])
    pad_on = jnp.concatenate([cnt > 0, tail_tiles < n_rows]).astype(jnp.int32)
    pad_start = jnp.clip(pad_start, 0, n_rows - GROUP_TILE).astype(jnp.int32)
    tile_start = jnp.arange(n_tiles, dtype=jnp.int32) * GROUP_TILE
    last_tile = jnp.maximum(ends[-1] // GROUP_TILE - 1, 0).astype(jnp.int32).reshape(1)
    tile_expert = jnp.sum(tile_start[:, None] >= ends[None, :], axis=1).astype(jnp.int32)
    tile_expert = jnp.minimum(tile_expert, tile_expert[last_tile[0]])

    xs = _dispatch(x2, g, pos_t, pad_start, pad_on, n_rows)
    ys = _group_ffn(xs, tile_expert, last_tile, wg, wu, wd)
    return _combine(x2, meta, pos_t, ys, final_g, final_norm)


def kernel(x, attn_norm_g, w_in, pool_w, pool_scale, conv_w, conv_b, conv_ln_g, conv_ln_b,
           group_norm_g, w_out, ffn_norm_g, dense_w_gate, dense_w_up, dense_w_down,
           router_w, moe_w_gate, moe_w_up, moe_w_down, final_norm_g):
    b, s, d = x.shape
    depth = w_in.shape[0]
    t = b * s
    w_in_bf16 = w_in[0].astype(BF16)
    for l in range(depth):
        x2 = x.reshape(t, d)
        wp_bd = jax.scipy.linalg.block_diag(*[pool_w[l, gi] for gi in range(pool_w.shape[1])]).astype(BF16)
        qkv, obc = _inproj_mix(x2, s, attn_norm_g[l], w_in_bf16, wp_bd, pool_scale[l], conv_w[l],
                               conv_b[l], conv_ln_g[l], conv_ln_b[l], group_norm_g[l])
        last = l == depth - 1
        i = l // 2
        ffn_f32 = ((dense_w_gate, dense_w_up, dense_w_down) if l % 2 == 0
                   else (moe_w_gate, moe_w_up, moe_w_down))
        jobs = [(w, i) for w in ffn_f32] + [(w_out, l)] + ([] if last else [(w_in, l + 1)])
        o_a, casts = _attention(qkv.reshape(b, s, -1), jobs)
        ffn_bf16, w_out_bf16 = casts[:3], casts[3]
        if not last:
            w_in_bf16 = casts[4]
        oa2 = o_a.reshape(t, -1)
        if l % 2 == 0:
            x2 = _outproj_dense(x2, oa2, obc, group_norm_g[l], w_out_bf16, ffn_norm_g[l], *ffn_bf16,
                                final_norm_g, last)
        else:
            x1, meta_t, counts = _outproj_router(x2, oa2, obc, group_norm_g[l], w_out_bf16,
                                                 ffn_norm_g[l], router_w[i].T)
            x2 = _moe_ffn(x1, meta_t, counts, ffn_norm_g[l], *ffn_bf16, final_norm_g, last)
        x = x2.reshape(b, s, d)
    return x
```

```python
import functools
import math

import jax
import jax.numpy as jnp
from jax import lax
from jax.experimental import pallas as pl
from jax.experimental.pallas import tpu as pltpu

F32 = jnp.float32
BF16 = jnp.bfloat16

LANES = 128
SUBLANES = 8

N_HEADS = 8
HEAD_DIM = 64
ATTN_W = N_HEADS * HEAD_DIM
POOL_W = 256
CONV_W = 256
POOL_WINDOWS = (2, 4, 8, 16)
CONV_K = 31
HALO = 32
N_EXPERTS = 8
TOP_K = 2
RMS_EPS = 1e-6
LN_EPS = 1e-5

ATTN_BLOCK = 256
EXP_ZERO_BELOW = -105.0

TOKEN_TILE = 512
ROW_CHUNK = 64
GROUP_TILE = 256
FF_CHUNKS = (1024, 1024, 768)
QKV_CHUNK = 256

VMEM_LIMIT = 56 * 1024 * 1024


def _cparams(n_axes, vmem=VMEM_LIMIT):
    return pltpu.CompilerParams(dimension_semantics=("arbitrary",) * n_axes, vmem_limit_bytes=vmem)


def _rms(x, g):
    return x * lax.rsqrt(jnp.mean(x * x, axis=-1, keepdims=True) + RMS_EPS) * g


def _dot(a, b):
    return jnp.dot(a, b, preferred_element_type=F32)


def _const_spec(shape):
    zeros = (0,) * len(shape)
    return pl.BlockSpec(shape, lambda *_: zeros, pipeline_mode=pl.Buffered(1))


def _sigmoid(a):
    return 1.0 / (1.0 + jnp.exp(-a))


def _inproj_mix_kernel(x_ref, *refs, tiles_per_seq):
    _inproj_mix_rows(x_ref[...], *refs, tiles_per_seq=tiles_per_seq)


def _inproj_mix_rows(x, g_ref, w_ref, wp_ref, ps_ref, cw_ref, cb_ref, lg_ref, lb_ref, gn_ref,
                     qkv_ref, obc_ref, pool_ext, conv_ext, conv_shift, *, tiles_per_seq):
    tm = x.shape[0]
    i = pl.program_id(0)
    in_sequence = lax.rem(i, tiles_per_seq) > 0
    ext_rows = HALO + tm

    @pl.when(i == 0)
    def _():
        pool_ext[...] = jnp.zeros_like(pool_ext)
        conv_ext[...] = jnp.zeros_like(conv_ext)

    h = _rms(x, g_ref[...]).astype(BF16)
    nq = qkv_ref.shape[-1]
    rest = _dot(h, w_ref[:, nq:])
    pool_ext[0:HALO, :] = jnp.where(in_sequence, pool_ext[tm:ext_rows, :], 0.0)
    conv_ext[0:HALO, :] = jnp.where(in_sequence, conv_ext[tm:ext_rows, :], 0.0)
    pool_ext[HALO:ext_rows, :] = rest[:, 0:POOL_W]
    conv_ext[HALO:ext_rows, :] = (rest[:, POOL_W:POOL_W + CONV_W]
                                  * _sigmoid(rest[:, POOL_W + CONV_W:]))

    qkv_chunks = iter(range(nq // QKV_CHUNK))

    def qkv_part():
        c = next(qkv_chunks, None)
        if c is not None:
            cols = slice(c * QKV_CHUNK, (c + 1) * QKV_CHUNK)
            qkv_ref[:, cols] = _dot(h, w_ref[:, cols]).astype(BF16)

    lane = lax.broadcasted_iota(jnp.int32, (1, LANES), 1)
    low_half = lane < (LANES // 2)
    rc = min(ROW_CHUNK, tm)
    seq_pos = lax.rem(i, tiles_per_seq) * tm
    pooled_rows = []
    for r0 in range(0, tm, rc):
        pos = seq_pos + r0 + lax.broadcasted_iota(jnp.int32, (rc, 1), 0)
        halves = []
        for c, (w_lo, w_hi) in enumerate(((POOL_WINDOWS[0], POOL_WINDOWS[1]),
                                          (POOL_WINDOWS[2], POOL_WINDOWS[3]))):
            cols = slice(c * LANES, (c + 1) * LANES)
            base = HALO + r0
            ident = pool_ext[base:base + rc, cols]
            s_lo = ident
            for j in range(1, w_lo):
                s_lo = s_lo + pool_ext[base - j:base - j + rc, cols]
            s_hi = s_lo
            for j in range(w_lo, w_hi):
                s_hi = s_hi + pool_ext[base - j:base - j + rc, cols]
            total = jnp.where(low_half, s_lo, s_hi)
            win = jnp.where(low_half, w_lo, w_hi)
            count = jnp.minimum(pos + 1, win).astype(F32)
            halves.append(total / count - ident)
        pooled_rows.append(jnp.concatenate(halves, axis=1))
        if r0 % (2 * rc) == 0:
            qkv_part()
    pooled = jnp.concatenate(pooled_rows, axis=0).astype(BF16)
    o_b = _dot(pooled, wp_ref[...]) * ps_ref[...]

    for s in range(1, SUBLANES):
        conv_shift[s - 1] = conv_ext[s:s + ext_rows, :]
    conv_rows = []
    for r0 in range(0, tm, rc):
        qkv_part()
        part = jnp.zeros((rc, CONV_W), F32) + cb_ref[...]
        for j in range(CONV_K):
            off = HALO - (CONV_K - 1) + j + r0
            s, aligned = off % SUBLANES, off - off % SUBLANES
            window = (conv_ext[aligned:aligned + rc, :] if s == 0
                      else conv_shift[s - 1, aligned:aligned + rc, :])
            part = part + cw_ref[j:j + 1, :] * window
        conv_rows.append(part)
    for _ in qkv_chunks:
        raise AssertionError("tile too small to place every q/k/v matmul chunk")
    conv = jnp.concatenate(conv_rows, axis=0)
    mu = jnp.mean(conv, axis=-1, keepdims=True)
    cen = conv - mu
    var = jnp.mean(cen * cen, axis=-1, keepdims=True)
    ln = cen * lax.rsqrt(var + LN_EPS) * lg_ref[...] + lb_ref[...]
    o_c = ln * _sigmoid(ln)

    gn = gn_ref[...]
    obc_ref[...] = jnp.concatenate([_rms(o_b, gn[:, ATTN_W:ATTN_W + POOL_W]),
                                    _rms(o_c, gn[:, ATTN_W + POOL_W:])], axis=1).astype(BF16)


def _inproj_mix_specs(t, d, n, tm):
    assert HALO >= CONV_K - 1 and HALO >= max(POOL_WINDOWS) - 1 and tm >= HALO
    nq = 3 * ATTN_W
    in_specs = [_const_spec((1, d)),
                _const_spec((d, n)),
                _const_spec((POOL_W, POOL_W)),
                _const_spec((1, POOL_W)),
                _const_spec((CONV_K, CONV_W)),
                _const_spec((1, CONV_W)),
                _const_spec((1, CONV_W)),
                _const_spec((1, CONV_W)),
                _const_spec((1, d))]
    out_specs = [pl.BlockSpec((tm, nq), lambda i: (i, 0)),
                 pl.BlockSpec((tm, POOL_W + CONV_W), lambda i: (i, 0))]
    out_shapes = [jax.ShapeDtypeStruct((t, nq), BF16),
                  jax.ShapeDtypeStruct((t, POOL_W + CONV_W), BF16)]
    scratch = [pltpu.VMEM((HALO + tm, POOL_W), F32),
               pltpu.VMEM((HALO + tm + SUBLANES, CONV_W), F32),
               pltpu.VMEM((SUBLANES - 1, HALO + tm, CONV_W), F32)]
    return in_specs, out_specs, out_shapes, scratch


def _inproj_mix(x2, seq_len, g, w_bf16, wp_bd, pool_scale, conv_w, conv_b, ln_g, ln_b, gn_g):
    t, d = x2.shape
    tm = min(TOKEN_TILE, seq_len)
    in_specs, out_specs, out_shapes, scratch = _inproj_mix_specs(t, d, w_bf16.shape[1], tm)
    row = lambda a: a.reshape(1, -1)
    return pl.pallas_call(
        functools.partial(_inproj_mix_kernel, tiles_per_seq=seq_len // tm),
        grid=(t // tm,),
        in_specs=[pl.BlockSpec((tm, d), lambda i: (i, 0))] + in_specs,
        out_specs=out_specs,
        out_shape=out_shapes,
        scratch_shapes=scratch,
        compiler_params=_cparams(1),
        name="inproj_mix",
    )(x2, row(g), w_bf16, wp_bd, row(pool_scale), conv_w, row(conv_b), row(ln_g), row(ln_b), row(gn_g))


def _attn_kernel(q_ref, k_ref, v_ref, o_ref, carry_ref):
    blk = q_ref.shape[1]
    n_pairs = q_ref.shape[2] // LANES
    qi = pl.program_id(1)
    scale = jnp.asarray(1.0 / math.sqrt(HEAD_DIM), BF16)

    lane = lax.broadcasted_iota(jnp.int32, (1, LANES), 1)
    head_lanes = (lane < HEAD_DIM, lane >= HEAD_DIM)
    zero_bf = jnp.zeros((), BF16)

    jj = lax.broadcasted_iota(jnp.int32, (blk, blk), 0)
    ss = lax.broadcasted_iota(jnp.int32, (blk, blk), 1)
    upper2 = jnp.where(jj > ss, -1.0, 0.0).astype(BF16)

    n_heads = 2 * n_pairs

    def add_blocks(blocks):
        starts = [pl.multiple_of(j * blk, blk) for j, _ in blocks]
        if any(diagonal for _, diagonal in blocks):
            r_idx = lax.broadcasted_iota(jnp.int32, (blk, blk), 0)
            c_idx = lax.broadcasted_iota(jnp.int32, (blk, blk), 1)
            causal = c_idx < r_idx
        n_chains = n_heads * len(blocks)

        def cols(c):
            n = c % n_heads
            return slice((n // 2) * LANES, (n // 2 + 1) * LANES)

        def scores(c):
            q = q_ref[0, :, cols(c)] * scale
            k = k_ref[0, pl.ds(starts[c // n_heads], blk), cols(c)]
            qh = jnp.where(head_lanes[c % 2], q, zero_bf)
            return lax.dot_general(qh, k, (((1,), (1,)), ((), ())), preferred_element_type=F32)

        def log_terms(c, z):
            softplus = jnp.maximum(z, 0.0) + jnp.log(1.0 + jnp.exp(jnp.minimum(z, -z)))
            if blocks[c // n_heads][1]:
                softplus = jnp.where(causal, softplus, 0.0)
            return z - softplus, softplus.astype(BF16), softplus[:, 0:1]

        def weighted_values(c, log_beta, rest, first_col):
            n, first = c % n_heads, c < n_heads and blocks[0][1]
            arg = log_beta + rest
            if not first:
                arg = arg + carry_ref[n]
            w = jnp.exp(arg)
            if blocks[c // n_heads][1]:
                w = jnp.where(causal, w, 0.0)
            v = v_ref[0, pl.ds(starts[c // n_heads], blk), cols(c)]
            vh = jnp.where(head_lanes[c % 2], v, zero_bf)
            block_sum = rest[:, 0:1] - first_col
            carry = block_sum if first else carry_ref[n] + block_sum
            carry_ref[n] = carry
            return _dot(w.astype(BF16), vh), carry

        z, terms, rest, pv, bound = {}, {}, {}, {}, None
        for step in range(n_chains + 3):
            c4, c3, c2, c1 = step - 3, step - 2, step - 1, step
            if 0 <= c3 < n_chains:
                rest[c3] = _dot(terms[c3][1], upper2)
            if 0 <= c1 < n_chains:
                z[c1] = scores(c1)
            if 0 <= c4 < n_chains:
                pv[c4], carry = weighted_values(c4, terms[c4][0], rest.pop(c4), terms[c4][2])
                del terms[c4]
                if c4 >= n_chains - n_heads:
                    bound = carry if bound is None else jnp.maximum(bound, carry)
                if c4 % 2 == 1:
                    both = pv.pop(c4 - 1) + pv.pop(c4)
                    first = c4 < n_heads and blocks[0][1]
                    o_ref[0, :, cols(c4)] = both if first else o_ref[0, :, cols(c4)] + both
            if 0 <= c2 < n_chains:
                terms[c2] = log_terms(c2, z.pop(c2))
        return (jnp.max(bound) > EXP_ZERO_BELOW).astype(jnp.int32)

    @pl.when(qi == 0)
    def _():
        add_blocks([(qi, True)])

    @pl.when(qi > 0)
    def _():
        go = add_blocks([(qi, True), (qi - 1, False)])

        def cond(state):
            return jnp.logical_and(state[0] >= 0, state[1] > 0)

        def body(state):
            return state[0] - 1, add_blocks([(state[0], False)])

        lax.while_loop(cond, body, (qi - 2, go))


def _attn_and_cast_kernel(*refs, n_cast):
    q_ref, k_ref, v_ref = refs[:3]
    srcs = refs[3:3 + n_cast]
    o_ref = refs[3 + n_cast]
    dsts = refs[4 + n_cast:4 + 2 * n_cast]
    carry_ref = refs[4 + 2 * n_cast]
    for src, dst in zip(srcs, dsts):
        dst[...] = src[...].astype(BF16)
    _attn_kernel(q_ref, k_ref, v_ref, o_ref, carry_ref)


BF16_ROWS = 16


def _attention(qkv3, cast_weights=()):
    b, s, _ = qkv3.shape
    blk = min(ATTN_BLOCK, s)
    n_q = s // blk
    steps = b * n_q
    cast_2d, in_cast, out_cast, out_shapes = [], [], [], []
    for w, index in cast_weights:
        cols = w.shape[-1]
        layer_rows = math.prod(w.shape[1:-1])
        hold = 1
        while steps % hold or layer_rows % (steps // hold) or (layer_rows // (steps // hold)) % BF16_ROWS:
            hold += 1
        n_blocks = steps // hold
        rows = layer_rows // n_blocks

        def in_map(bi, qi, index=index, hold=hold, n_blocks=n_blocks):
            return index * n_blocks + (bi * n_q + qi) // hold, 0

        def out_map(bi, qi, hold=hold):
            return (bi * n_q + qi) // hold, 0

        cast_2d.append(w.reshape(-1, cols))
        in_cast.append(pl.BlockSpec((rows, cols), in_map))
        out_cast.append(pl.BlockSpec((rows, cols), out_map))
        out_shapes.append(jax.ShapeDtypeStruct((layer_rows, cols), BF16))
    outs = pl.pallas_call(
        functools.partial(_attn_and_cast_kernel, n_cast=len(cast_2d)),
        grid=(b, n_q),
        in_specs=[pl.BlockSpec((1, blk, ATTN_W), lambda bi, qi: (bi, qi, 0)),
                  pl.BlockSpec((1, s, ATTN_W), lambda bi, qi: (bi, 0, 1)),
                  pl.BlockSpec((1, s, ATTN_W), lambda bi, qi: (bi, 0, 2))] + in_cast,
        out_specs=[pl.BlockSpec((1, blk, ATTN_W), lambda bi, qi: (bi, qi, 0))] + out_cast,
        out_shape=[jax.ShapeDtypeStruct((b, s, ATTN_W), F32)] + out_shapes,
        scratch_shapes=[pltpu.VMEM((N_HEADS, blk, 1), F32)],
        compiler_params=_cparams(2),
        name="sb_attention",
    )(qkv3, qkv3, qkv3, *cast_2d)
    return outs[0], [o.reshape(w.shape[1:]) for o, (w, _) in zip(outs[1:], cast_weights)]


def _outproj_rows(x_ref, oa_ref, obc_ref, gn_ref, wo_ref):
    oa = _rms(oa_ref[...], gn_ref[:, 0:ATTN_W]).astype(BF16)
    o = jnp.concatenate([oa, obc_ref[...]], axis=1)
    return x_ref[...] + _dot(o, wo_ref[...])


def _swiglu_rows(h_bf16, wg_ref, wu_ref, wd_ref, acc):
    f0 = 0
    for fc in FF_CHUNKS:
        gate = _dot(h_bf16, wg_ref[:, f0:f0 + fc])
        up = _dot(h_bf16, wu_ref[:, f0:f0 + fc])
        act = (gate * _sigmoid(gate) * up).astype(BF16)
        acc = acc + _dot(act, wd_ref[f0:f0 + fc, :])
        f0 += fc
    return acc


def _outproj_dense_kernel(x_ref, oa_ref, obc_ref, gn_ref, wo_ref, g_ref, wg_ref, wu_ref, wd_ref,
                          fg_ref, out_ref, *, final_norm):
    x = _outproj_rows(x_ref, oa_ref, obc_ref, gn_ref, wo_ref)
    h = _rms(x, g_ref[...]).astype(BF16)
    y = _swiglu_rows(h, wg_ref, wu_ref, wd_ref, x)
    if final_norm:
        y = _rms(y, fg_ref[...])
    out_ref[...] = y


def _outproj_dense(x2, oa2, obc, gn_g, wo, g, wg, wu, wd, final_g, final_norm):
    t, d = x2.shape
    f = wg.shape[1]
    assert sum(FF_CHUNKS) == f
    tm = min(TOKEN_TILE, t)
    row = lambda a: a.reshape(1, -1)
    return pl.pallas_call(
        functools.partial(_outproj_dense_kernel, final_norm=final_norm),
        grid=(t // tm,),
        in_specs=[pl.BlockSpec((tm, d), lambda i: (i, 0)),
                  pl.BlockSpec((tm, ATTN_W), lambda i: (i, 0)),
                  pl.BlockSpec((tm, POOL_W + CONV_W), lambda i: (i, 0)),
                  _const_spec((1, d)),
                  _const_spec((d, d)),
                  _const_spec((1, d)),
                  _const_spec((d, f)),
                  _const_spec((d, f)),
                  _const_spec((f, d)),
                  _const_spec((1, d))],
        out_specs=pl.BlockSpec((tm, d), lambda i: (i, 0)),
        out_shape=jax.ShapeDtypeStruct((t, d), F32),
        compiler_params=_cparams(1),
        name="outproj_dense_ffn",
    )(x2, oa2, obc, row(gn_g), wo, row(g), wg, wu, wd, row(final_g))


META_E0, META_E1, META_G0, META_G1, META_R0, META_R1 = range(6)


def _split3(a):
    p0 = a.astype(BF16)
    r1 = a - p0.astype(F32)
    p1 = r1.astype(BF16)
    p2 = (r1 - p1.astype(F32)).astype(BF16)
    return p0, p1, p2


def _outproj_router_kernel(x_ref, oa_ref, obc_ref, gn_ref, wo_ref, g_ref, wrt_ref,
                           x1_ref, meta_ref, counts_ref, run_ref):
    tm = x_ref.shape[0]
    i = pl.program_id(0)
    nt_dims = (((1,), (1,)), ((), ()))

    @pl.when(i == 0)
    def _():
        run_ref[...] = jnp.zeros_like(run_ref)

    x1 = _outproj_rows(x_ref, oa_ref, obc_ref, gn_ref, wo_ref)
    x1_ref[...] = x1
    h = _rms(x1, g_ref[...])
    hs = _split3(h)
    ws = [p.astype(F32) for p in _split3(wrt_ref[...])]
    logits = jnp.zeros((N_EXPERTS, tm), F32)
    for a in (2, 1, 0):
        stacked = jnp.concatenate(ws[:3 - a], axis=0).astype(BF16)
        part = lax.dot_general(stacked, hs[a], nt_dims, preferred_element_type=F32)
        for b in range(3 - a):
            logits = logits + part[b * N_EXPERTS:(b + 1) * N_EXPERTS, :]

    eidx = lax.broadcasted_iota(jnp.int32, (N_EXPERTS, tm), 0).astype(F32)
    neg = jnp.float32(-jnp.inf)
    v0 = jnp.max(logits, axis=0, keepdims=True)
    e0 = jnp.min(jnp.where(logits == v0, eidx, float(N_EXPERTS)), axis=0, keepdims=True)
    masked = jnp.where(eidx == e0, neg, logits)
    v1 = jnp.max(masked, axis=0, keepdims=True)
    e1 = jnp.min(jnp.where(masked == v1, eidx, float(N_EXPERTS)), axis=0, keepdims=True)
    ex = jnp.exp(v1 - v0)
    g0 = 1.0 / (1.0 + ex)
    g1 = ex / (1.0 + ex)

    sel0 = eidx == e0
    sel1 = eidx == e1
    onehot = jnp.where(jnp.logical_or(sel0, sel1), 1.0, 0.0)
    rr = lax.broadcasted_iota(jnp.int32, (tm, tm), 0)
    cc = lax.broadcasted_iota(jnp.int32, (tm, tm), 1)
    earlier = jnp.where(rr < cc, 1.0, 0.0).astype(BF16)
    run = run_ref[:, 0:1]
    rank_all = _dot(onehot.astype(BF16), earlier) + run
    r0 = jnp.sum(jnp.where(sel0, rank_all, 0.0), axis=0, keepdims=True)
    r1 = jnp.sum(jnp.where(sel1, rank_all, 0.0), axis=0, keepdims=True)
    new_run = run + jnp.sum(onehot, axis=1, keepdims=True)
    run_ref[...] = jnp.broadcast_to(new_run, run_ref.shape)
    counts_ref[...] = jnp.broadcast_to(new_run, counts_ref.shape)

    zero_row = jnp.zeros((1, tm), F32)
    meta_ref[...] = jnp.concatenate([e0, e1, g0, g1, r0, r1, zero_row, zero_row], axis=0)


def _outproj_router(x2, oa2, obc, gn_g, wo, g, wr_t):
    t, d = x2.shape
    tm = min(TOKEN_TILE, t)
    row = lambda a: a.reshape(1, -1)
    return pl.pallas_call(
        _outproj_router_kernel,
        grid=(t // tm,),
        in_specs=[pl.BlockSpec((tm, d), lambda i: (i, 0)),
                  pl.BlockSpec((tm, ATTN_W), lambda i: (i, 0)),
                  pl.BlockSpec((tm, POOL_W + CONV_W), lambda i: (i, 0)),
                  _const_spec((1, d)),
                  _const_spec((d, d)),
                  _const_spec((1, d)),
                  _const_spec((N_EXPERTS, d))],
        out_specs=[pl.BlockSpec((tm, d), lambda i: (i, 0)),
                   pl.BlockSpec((SUBLANES, tm), lambda i: (0, i)),
                   pl.BlockSpec((N_EXPERTS, LANES), lambda i: (0, 0))],
        out_shape=[jax.ShapeDtypeStruct((t, d), F32),
                   jax.ShapeDtypeStruct((SUBLANES, t), F32),
                   jax.ShapeDtypeStruct((N_EXPERTS, LANES), F32)],
        scratch_shapes=[pltpu.VMEM((N_EXPERTS, LANES), F32)],
        compiler_params=_cparams(1),
        name="outproj_router",
    )(x2, oa2, obc, row(gn_g), wo, row(g), wr_t)


ROWS_PER_ISSUE = 8


def _row_view(ref, row):
    return ref.at[pl.ds(pl.multiple_of(row * SUBLANES, SUBLANES), SUBLANES), :]


def _dispatch_kernel(pos_ref, pad_start_ref, pad_on_ref, x_ref, g_ref, xs_ref, rows, zeros, sems, zsem):
    tm = x_ref.shape[0]
    i = pl.program_id(0)
    n = pl.num_programs(0)
    slot = lax.rem(i, 2)
    tile_rows = tm * SUBLANES

    def drain(s):
        for _ in range(TOP_K):
            pltpu.make_async_copy(rows.at[s], xs_ref.at[pl.ds(0, tile_rows), :], sems.at[s]).wait()

    @pl.when(i == 0)
    def _():
        zeros[...] = jnp.zeros_like(zeros)
        zrows = zeros.shape[0]
        for e in range(pad_start_ref.shape[0]):
            @pl.when(pad_on_ref[e] > 0)
            def _():
                start = pl.multiple_of(pad_start_ref[e] * SUBLANES, SUBLANES)
                pltpu.make_async_copy(zeros, xs_ref.at[pl.ds(start, zrows), :], zsem).start()
        for e in range(pad_start_ref.shape[0]):
            @pl.when(pad_on_ref[e] > 0)
            def _():
                pltpu.make_async_copy(zeros, xs_ref.at[pl.ds(0, zrows), :], zsem).wait()

    @pl.when(i >= 2)
    def _():
        drain(slot)

    h = _rms(x_ref[...], g_ref[...])
    buf = rows.at[slot]
    for j in range(h.shape[1] // LANES):
        buf[pl.ds(j, tm, stride=SUBLANES), :] = h[:, j * LANES:(j + 1) * LANES]

    def issue(c, _):
        for u in range(ROWS_PER_ISSUE):
            r = c * ROWS_PER_ISSUE + u
            for k in range(TOP_K):
                dst = pos_ref[0, 0, k * tm + r]
                pltpu.make_async_copy(_row_view(buf, r), _row_view(xs_ref, dst),
                                      sems.at[slot]).start(priority=k)
        return 0

    lax.fori_loop(0, tm // ROWS_PER_ISSUE, issue, 0)

    @pl.when(i == n - 1)
    def _():
        @pl.when(n >= 2)
        def _():
            drain(1 - slot)
        drain(slot)


def _tile_positions(pos_t, tm):
    n_tiles = pos_t.shape[1] // tm
    return pos_t.reshape(TOP_K, n_tiles, tm).transpose(1, 0, 2).reshape(n_tiles, 1, TOP_K * tm)


def _dispatch(x2, g, pos_t, pad_start, pad_on, n_rows):
    t, d = x2.shape
    tm = min(TOKEN_TILE, t)
    n_tiles = t // tm
    pos3 = _tile_positions(pos_t, tm)
    grid_spec = pltpu.PrefetchScalarGridSpec(
        num_scalar_prefetch=0,
        grid=(n_tiles,),
        in_specs=[pl.BlockSpec((1, 1, tm * TOP_K), lambda i: (i, 0, 0), memory_space=pltpu.SMEM),
                  pl.BlockSpec(memory_space=pltpu.SMEM),
                  pl.BlockSpec(memory_space=pltpu.SMEM),
                  pl.BlockSpec((tm, d), lambda i: (i, 0)),
                  _const_spec((1, d))],
        out_specs=pl.BlockSpec(memory_space=pl.ANY),
        scratch_shapes=[pltpu.VMEM((2, tm * SUBLANES, LANES), F32),
                        pltpu.VMEM((GROUP_TILE * SUBLANES, LANES), F32),
                        pltpu.SemaphoreType.DMA((2,)),
                        pltpu.SemaphoreType.DMA(())],
    )
    return pl.pallas_call(
        _dispatch_kernel,
        grid_spec=grid_spec,
        out_shape=jax.ShapeDtypeStruct((n_rows * SUBLANES, LANES), F32),
        compiler_params=_cparams(1),
        name="moe_dispatch",
    )(pos3, pad_start, pad_on, x2, g.reshape(1, d))


def _group_ffn_kernel(te_ref, last_ref, xs_ref, wg_ref, wu_ref, wd_ref, ys_ref):
    i = pl.program_id(0)
    tg = xs_ref.shape[0] // SUBLANES
    d = wg_ref.shape[1]

    @pl.when(i <= last_ref[0])
    def _():
        x = jnp.concatenate([xs_ref[pl.ds(j, tg, stride=SUBLANES), :] for j in range(d // LANES)],
                            axis=1).astype(BF16)
        y = _swiglu_rows(x, wg_ref.at[0], wu_ref.at[0], wd_ref.at[0], jnp.zeros((tg, d), F32))
        for j in range(d // LANES):
            ys_ref[pl.ds(j, tg, stride=SUBLANES), :] = y[:, j * LANES:(j + 1) * LANES]

    @pl.when(i > last_ref[0])
    def _():
        ys_ref[...] = jnp.zeros_like(ys_ref)


def _group_ffn(xs, tile_expert, last_tile, wg, wu, wd):
    n_tiles = tile_expert.shape[0]
    _, d, f = wg.shape
    blk = GROUP_TILE * SUBLANES
    grid_spec = pltpu.PrefetchScalarGridSpec(
        num_scalar_prefetch=2,
        grid=(n_tiles,),
        in_specs=[pl.BlockSpec((blk, LANES), lambda i, te, last: (jnp.minimum(i, last[0]), 0)),
                  pl.BlockSpec((1, d, f), lambda i, te, last: (te[i], 0, 0)),
                  pl.BlockSpec((1, d, f), lambda i, te, last: (te[i], 0, 0)),
                  pl.BlockSpec((1, f, d), lambda i, te, last: (te[i], 0, 0))],
        out_specs=pl.BlockSpec((blk, LANES), lambda i, te, last: (i, 0)),
    )
    return pl.pallas_call(
        _group_ffn_kernel,
        grid_spec=grid_spec,
        out_shape=jax.ShapeDtypeStruct(xs.shape, F32),
        compiler_params=_cparams(1),
        name="moe_group_ffn",
    )(tile_expert, last_tile, xs, wg, wu, wd)


def _combine_rows(pos_ref, pos_next_ref, x_ref, meta_ref, ys_ref, bufs, sems):
    tm = x_ref.shape[0]
    i = pl.program_id(0)
    n = pl.num_programs(0)
    slot = lax.rem(i, 2)
    tile_rows = tm * SUBLANES

    def fetch(p_ref, s):
        def issue(c, _):
            for u in range(ROWS_PER_ISSUE):
                r = c * ROWS_PER_ISSUE + u
                for k in range(TOP_K):
                    src = p_ref[0, 0, k * tm + r]
                    pltpu.make_async_copy(_row_view(ys_ref, src), _row_view(bufs.at[s, k], r),
                                          sems.at[s]).start(priority=k)
            return 0
        lax.fori_loop(0, tm // ROWS_PER_ISSUE, issue, 0)

    @pl.when(i == 0)
    def _():
        fetch(pos_ref, 0)

    @pl.when(i + 1 < n)
    def _():
        fetch(pos_next_ref, 1 - slot)

    for k in range(TOP_K):
        pltpu.make_async_copy(ys_ref.at[pl.ds(0, tile_rows), :], bufs.at[slot, k], sems.at[slot]).wait()

    meta = meta_ref[...]
    g0 = meta[:, META_G0:META_G0 + 1]
    g1 = meta[:, META_G1:META_G1 + 1]
    x = x_ref[...]
    cols = []
    for j in range(x.shape[1] // LANES):
        y0 = bufs[slot, 0, pl.ds(j, tm, stride=SUBLANES), :]
        y1 = bufs[slot, 1, pl.ds(j, tm, stride=SUBLANES), :]
        cols.append(x[:, j * LANES:(j + 1) * LANES] + (g0 * y0 + g1 * y1))
    return jnp.concatenate(cols, axis=1)


def _combine_kernel(pos_ref, pos_next_ref, x_ref, meta_ref, fg_ref, ys_ref, out_ref, bufs, sems):
    y = _combine_rows(pos_ref, pos_next_ref, x_ref, meta_ref, ys_ref, bufs, sems)
    out_ref[...] = _rms(y, fg_ref[...])


def _combine_inproj_mix_kernel(pos_ref, pos_next_ref, x_ref, meta_ref, ys_ref, *refs, tiles_per_seq):
    n_mix_in = 9
    x_out_ref, qkv_ref, obc_ref = refs[n_mix_in:n_mix_in + 3]
    bufs, sems = refs[n_mix_in + 3:n_mix_in + 5]
    mix_scratch = refs[n_mix_in + 5:]
    x = _combine_rows(pos_ref, pos_next_ref, x_ref, meta_ref, ys_ref, bufs, sems)
    x_out_ref[...] = x
    _inproj_mix_rows(x, *refs[:n_mix_in], qkv_ref, obc_ref, *mix_scratch, tiles_per_seq=tiles_per_seq)


def _combine_specs(t, d, tm):
    n_tiles = t // tm
    in_specs = [pl.BlockSpec((1, 1, tm * TOP_K), lambda i: (i, 0, 0), memory_space=pltpu.SMEM),
                pl.BlockSpec((1, 1, tm * TOP_K), lambda i: (jnp.minimum(i + 1, n_tiles - 1), 0, 0),
                             memory_space=pltpu.SMEM),
                pl.BlockSpec((tm, d), lambda i: (i, 0)),
                pl.BlockSpec((tm, SUBLANES), lambda i: (i, 0))]
    scratch = [pltpu.VMEM((2, TOP_K, tm * SUBLANES, LANES), F32), pltpu.SemaphoreType.DMA((2,))]
    return in_specs, scratch


def _combine(x2, meta, pos_t, ys, final_g):
    t, d = x2.shape
    tm = min(GROUP_TILE, t)
    pos3 = _tile_positions(pos_t, tm)
    in_specs, scratch = _combine_specs(t, d, tm)
    return pl.pallas_call(
        _combine_kernel,
        grid=(t // tm,),
        in_specs=in_specs + [_const_spec((1, d)), pl.BlockSpec(memory_space=pl.ANY)],
        out_specs=pl.BlockSpec((tm, d), lambda i: (i, 0)),
        out_shape=jax.ShapeDtypeStruct((t, d), F32),
        scratch_shapes=scratch,
        compiler_params=_cparams(1),
        name="moe_combine",
    )(pos3, pos3, x2, meta, final_g.reshape(1, d), ys)


def _combine_inproj_mix(x2, meta, pos_t, ys, seq_len, g, w_bf16, wp_bd, pool_scale, conv_w, conv_b,
                        ln_g, ln_b, gn_g):
    t, d = x2.shape
    tm = min(TOKEN_TILE, seq_len)
    pos3 = _tile_positions(pos_t, tm)
    in_specs, scratch = _combine_specs(t, d, tm)
    mix_in, mix_out, mix_shapes, mix_scratch = _inproj_mix_specs(t, d, w_bf16.shape[1], tm)
    row = lambda a: a.reshape(1, -1)
    return pl.pallas_call(
        functools.partial(_combine_inproj_mix_kernel, tiles_per_seq=seq_len // tm),
        grid=(t // tm,),
        in_specs=in_specs + [pl.BlockSpec(memory_space=pl.ANY)] + mix_in,
        out_specs=[pl.BlockSpec((tm, d), lambda i: (i, 0))] + mix_out,
        out_shape=[jax.ShapeDtypeStruct((t, d), F32)] + mix_shapes,
        scratch_shapes=scratch + mix_scratch,
        compiler_params=_cparams(1),
        name="combine_inproj_mix",
    )(pos3, pos3, x2, meta, ys, row(g), w_bf16, wp_bd, row(pool_scale), conv_w, row(conv_b),
      row(ln_g), row(ln_b), row(gn_g))


def _moe_expert_rows(x2, meta_t, counts, g, wg, wu, wd):
    t, d = x2.shape
    meta = meta_t.T

    cnt = counts[:, 0].astype(jnp.int32)
    padded = ((cnt + GROUP_TILE - 1) // GROUP_TILE) * GROUP_TILE
    ends = jnp.cumsum(padded)
    offs = ends - padded
    n_tiles = (t * TOP_K) // GROUP_TILE + N_EXPERTS
    n_rows = n_tiles * GROUP_TILE
    eid_t = meta_t[META_E0:META_E1 + 1].astype(jnp.int32)
    rank_t = meta_t[META_R0:META_R1 + 1].astype(jnp.int32)
    pos_t = rank_t
    for e in range(N_EXPERTS):
        pos_t = pos_t + jnp.where(eid_t == e, offs[e], 0)
    tail_tiles = ends[-1] + jnp.arange(N_EXPERTS, dtype=jnp.int32) * GROUP_TILE
    pad_start = jnp.concatenate([ends - GROUP_TILE, tail_tiles])
    pad_on = jnp.concatenate([cnt > 0, tail_tiles < n_rows]).astype(jnp.int32)
    pad_start = jnp.clip(pad_start, 0, n_rows - GROUP_TILE).astype(jnp.int32)
    tile_start = jnp.arange(n_tiles, dtype=jnp.int32) * GROUP_TILE
    last_tile = jnp.maximum(ends[-1] // GROUP_TILE - 1, 0).astype(jnp.int32).reshape(1)
    tile_expert = jnp.sum(tile_start[:, None] >= ends[None, :], axis=1).astype(jnp.int32)
    tile_expert = jnp.minimum(tile_expert, tile_expert[last_tile[0]])

    xs = _dispatch(x2, g, pos_t, pad_start, pad_on, n_rows)
    ys = _group_ffn(xs, tile_expert, last_tile, wg, wu, wd)
    return meta, pos_t, ys


def kernel(x, attn_norm_g, w_in, pool_w, pool_scale, conv_w, conv_b, conv_ln_g, conv_ln_b,
           group_norm_g, w_out, ffn_norm_g, dense_w_gate, dense_w_up, dense_w_down,
           router_w, moe_w_gate, moe_w_up, moe_w_down, final_norm_g):
    b, s, d = x.shape
    depth = w_in.shape[0]
    t = b * s
    def mix_params(l, w_in_bf16):
        wp_bd = jax.scipy.linalg.block_diag(*[pool_w[l, gi] for gi in range(pool_w.shape[1])]).astype(BF16)
        return (attn_norm_g[l], w_in_bf16, wp_bd, pool_scale[l], conv_w[l], conv_b[l], conv_ln_g[l],
                conv_ln_b[l], group_norm_g[l])

    x2 = x.reshape(t, d)
    qkv, obc = _inproj_mix(x2, s, *mix_params(0, w_in[0].astype(BF16)))
    for l in range(depth):
        last = l == depth - 1
        i = l // 2
        ffn_f32 = ((dense_w_gate, dense_w_up, dense_w_down) if l % 2 == 0
                   else (moe_w_gate, moe_w_up, moe_w_down))
        jobs = [(w, i) for w in ffn_f32] + [(w_out, l)] + ([] if last else [(w_in, l + 1)])
        o_a, casts = _attention(qkv.reshape(b, s, -1), jobs)
        ffn_bf16, w_out_bf16 = casts[:3], casts[3]
        next_mix = None if last else mix_params(l + 1, casts[4])
        oa2 = o_a.reshape(t, -1)
        if l % 2 == 0:
            x2 = _outproj_dense(x2, oa2, obc, group_norm_g[l], w_out_bf16, ffn_norm_g[l], *ffn_bf16,
                                final_norm_g, last)
            if not last:
                qkv, obc = _inproj_mix(x2, s, *next_mix)
        else:
            x1, meta_t, counts = _outproj_router(x2, oa2, obc, group_norm_g[l], w_out_bf16,
                                                 ffn_norm_g[l], router_w[i].T)
            meta, pos_t, ys = _moe_expert_rows(x1, meta_t, counts, ffn_norm_g[l], *ffn_bf16)
            if last:
                x2 = _combine(x1, meta, pos_t, ys, final_norm_g)
            else:
                x2, qkv, obc = _combine_inproj_mix(x1, meta, pos_t, ys, s, *next_mix)
    return x2.reshape(b, s, d)
```

```python
import functools
import math

import jax
import jax.numpy as jnp
from jax import lax
from jax.experimental import pallas as pl
from jax.experimental.pallas import tpu as pltpu

F32 = jnp.float32
BF16 = jnp.bfloat16

LANES = 128
SUBLANES = 8

N_HEADS = 8
HEAD_DIM = 64
ATTN_W = N_HEADS * HEAD_DIM
POOL_W = 256
CONV_W = 256
POOL_WINDOWS = (2, 4, 8, 16)
CONV_K = 31
HALO = 32
N_EXPERTS = 8
TOP_K = 2
RMS_EPS = 1e-6
LN_EPS = 1e-5

ATTN_BLOCK = 256
EXP_ZERO_BELOW = -105.0

TOKEN_TILE = 512
ROW_CHUNK = 64
GROUP_TILE = 256
FF_CHUNKS = (1024, 1024, 768)
QKV_CHUNK = 256

VMEM_LIMIT = 56 * 1024 * 1024


def _cparams(n_axes, vmem=VMEM_LIMIT):
    return pltpu.CompilerParams(dimension_semantics=("arbitrary",) * n_axes, vmem_limit_bytes=vmem)


def _rms(x, g):
    return x * lax.rsqrt(jnp.mean(x * x, axis=-1, keepdims=True) + RMS_EPS) * g


def _dot(a, b):
    return jnp.dot(a, b, preferred_element_type=F32)


def _const_spec(shape):
    zeros = (0,) * len(shape)
    return pl.BlockSpec(shape, lambda *_: zeros, pipeline_mode=pl.Buffered(1))


def _sigmoid(a):
    return 1.0 / (1.0 + jnp.exp(-a))


def _inproj_mix_kernel(x_ref, *refs, tiles_per_seq):
    _inproj_mix_rows(x_ref[...], *refs, tiles_per_seq=tiles_per_seq)


def _inproj_mix_rows(x, g_ref, w_ref, wp_ref, ps_ref, cw_ref, cb_ref, lg_ref, lb_ref, gn_ref,
                     qkv_ref, obc_ref, pool_ext, conv_ext, conv_shift, *, tiles_per_seq,
                     between_stages=lambda: None):
    tm = x.shape[0]
    i = pl.program_id(0)
    in_sequence = lax.rem(i, tiles_per_seq) > 0
    ext_rows = HALO + tm

    @pl.when(i == 0)
    def _():
        pool_ext[...] = jnp.zeros_like(pool_ext)
        conv_ext[...] = jnp.zeros_like(conv_ext)

    h = _rms(x, g_ref[...]).astype(BF16)
    nq = qkv_ref.shape[-1]
    rest = _dot(h, w_ref[:, nq:])
    pool_ext[0:HALO, :] = jnp.where(in_sequence, pool_ext[tm:ext_rows, :], 0.0)
    conv_ext[0:HALO, :] = jnp.where(in_sequence, conv_ext[tm:ext_rows, :], 0.0)
    pool_ext[HALO:ext_rows, :] = rest[:, 0:POOL_W]
    conv_ext[HALO:ext_rows, :] = (rest[:, POOL_W:POOL_W + CONV_W]
                                  * _sigmoid(rest[:, POOL_W + CONV_W:]))

    qkv_chunks = iter(range(nq // QKV_CHUNK))

    def qkv_part():
        c = next(qkv_chunks, None)
        if c is not None:
            cols = slice(c * QKV_CHUNK, (c + 1) * QKV_CHUNK)
            qkv_ref[:, cols] = _dot(h, w_ref[:, cols]).astype(BF16)

    lane = lax.broadcasted_iota(jnp.int32, (1, LANES), 1)
    low_half = lane < (LANES // 2)
    rc = min(ROW_CHUNK, tm)
    seq_pos = lax.rem(i, tiles_per_seq) * tm
    pooled_rows = []
    for r0 in range(0, tm, rc):
        pos = seq_pos + r0 + lax.broadcasted_iota(jnp.int32, (rc, 1), 0)
        halves = []
        for c, (w_lo, w_hi) in enumerate(((POOL_WINDOWS[0], POOL_WINDOWS[1]),
                                          (POOL_WINDOWS[2], POOL_WINDOWS[3]))):
            cols = slice(c * LANES, (c + 1) * LANES)
            base = HALO + r0
            ident = pool_ext[base:base + rc, cols]
            s_lo = ident
            for j in range(1, w_lo):
                s_lo = s_lo + pool_ext[base - j:base - j + rc, cols]
            s_hi = s_lo
            for j in range(w_lo, w_hi):
                s_hi = s_hi + pool_ext[base - j:base - j + rc, cols]
            total = jnp.where(low_half, s_lo, s_hi)
            win = jnp.where(low_half, w_lo, w_hi)
            count = jnp.minimum(pos + 1, win).astype(F32)
            halves.append(total / count - ident)
        pooled_rows.append(jnp.concatenate(halves, axis=1))
        between_stages()
        if r0 % (2 * rc) == 0:
            qkv_part()
    pooled = jnp.concatenate(pooled_rows, axis=0).astype(BF16)
    o_b = _dot(pooled, wp_ref[...]) * ps_ref[...]

    for s in range(1, SUBLANES):
        conv_shift[s - 1] = conv_ext[s:s + ext_rows, :]
    conv_rows = []
    for r0 in range(0, tm, rc):
        qkv_part()
        part = jnp.zeros((rc, CONV_W), F32) + cb_ref[...]
        for j in range(CONV_K):
            off = HALO - (CONV_K - 1) + j + r0
            s, aligned = off % SUBLANES, off - off % SUBLANES
            window = (conv_ext[aligned:aligned + rc, :] if s == 0
                      else conv_shift[s - 1, aligned:aligned + rc, :])
            part = part + cw_ref[j:j + 1, :] * window
        conv_rows.append(part)
        between_stages()
    for _ in qkv_chunks:
        raise AssertionError("tile too small to place every q/k/v matmul chunk")
    conv = jnp.concatenate(conv_rows, axis=0)
    mu = jnp.mean(conv, axis=-1, keepdims=True)
    cen = conv - mu
    var = jnp.mean(cen * cen, axis=-1, keepdims=True)
    ln = cen * lax.rsqrt(var + LN_EPS) * lg_ref[...] + lb_ref[...]
    o_c = ln * _sigmoid(ln)

    gn = gn_ref[...]
    obc_ref[...] = jnp.concatenate([_rms(o_b, gn[:, ATTN_W:ATTN_W + POOL_W]),
                                    _rms(o_c, gn[:, ATTN_W + POOL_W:])], axis=1).astype(BF16)


def _inproj_mix_specs(t, d, n, tm):
    assert HALO >= CONV_K - 1 and HALO >= max(POOL_WINDOWS) - 1 and tm >= HALO
    nq = 3 * ATTN_W
    in_specs = [_const_spec((1, d)),
                _const_spec((d, n)),
                _const_spec((POOL_W, POOL_W)),
                _const_spec((1, POOL_W)),
                _const_spec((CONV_K, CONV_W)),
                _const_spec((1, CONV_W)),
                _const_spec((1, CONV_W)),
                _const_spec((1, CONV_W)),
                _const_spec((1, d))]
    out_specs = [pl.BlockSpec((tm, nq), lambda i: (i, 0)),
                 pl.BlockSpec((tm, POOL_W + CONV_W), lambda i: (i, 0))]
    out_shapes = [jax.ShapeDtypeStruct((t, nq), BF16),
                  jax.ShapeDtypeStruct((t, POOL_W + CONV_W), BF16)]
    scratch = [pltpu.VMEM((HALO + tm, POOL_W), F32),
               pltpu.VMEM((HALO + tm + SUBLANES, CONV_W), F32),
               pltpu.VMEM((SUBLANES - 1, HALO + tm, CONV_W), F32)]
    return in_specs, out_specs, out_shapes, scratch


def _inproj_mix(x2, seq_len, g, w_bf16, wp_bd, pool_scale, conv_w, conv_b, ln_g, ln_b, gn_g):
    t, d = x2.shape
    tm = min(TOKEN_TILE, seq_len)
    in_specs, out_specs, out_shapes, scratch = _inproj_mix_specs(t, d, w_bf16.shape[1], tm)
    row = lambda a: a.reshape(1, -1)
    return pl.pallas_call(
        functools.partial(_inproj_mix_kernel, tiles_per_seq=seq_len // tm),
        grid=(t // tm,),
        in_specs=[pl.BlockSpec((tm, d), lambda i: (i, 0))] + in_specs,
        out_specs=out_specs,
        out_shape=out_shapes,
        scratch_shapes=scratch,
        compiler_params=_cparams(1),
        name="inproj_mix",
    )(x2, row(g), w_bf16, wp_bd, row(pool_scale), conv_w, row(conv_b), row(ln_g), row(ln_b), row(gn_g))


def _attn_kernel(q_ref, k_ref, v_ref, o_ref, carry_ref):
    blk = q_ref.shape[1]
    n_pairs = q_ref.shape[2] // LANES
    qi = pl.program_id(1)
    scale = jnp.asarray(1.0 / math.sqrt(HEAD_DIM), BF16)

    lane = lax.broadcasted_iota(jnp.int32, (1, LANES), 1)
    head_lanes = (lane < HEAD_DIM, lane >= HEAD_DIM)
    zero_bf = jnp.zeros((), BF16)

    jj = lax.broadcasted_iota(jnp.int32, (blk, blk), 0)
    ss = lax.broadcasted_iota(jnp.int32, (blk, blk), 1)
    upper2 = jnp.where(jj > ss, -1.0, 0.0).astype(BF16)

    n_heads = 2 * n_pairs

    def add_blocks(blocks):
        starts = [pl.multiple_of(j * blk, blk) for j, _ in blocks]
        if any(diagonal for _, diagonal in blocks):
            r_idx = lax.broadcasted_iota(jnp.int32, (blk, blk), 0)
            c_idx = lax.broadcasted_iota(jnp.int32, (blk, blk), 1)
            causal = c_idx < r_idx
        n_chains = n_heads * len(blocks)

        def cols(c):
            n = c % n_heads
            return slice((n // 2) * LANES, (n // 2 + 1) * LANES)

        def scores(c):
            q = q_ref[0, :, cols(c)] * scale
            k = k_ref[0, pl.ds(starts[c // n_heads], blk), cols(c)]
            qh = jnp.where(head_lanes[c % 2], q, zero_bf)
            return lax.dot_general(qh, k, (((1,), (1,)), ((), ())), preferred_element_type=F32)

        def log_terms(c, z):
            softplus = jnp.maximum(z, 0.0) + jnp.log(1.0 + jnp.exp(jnp.minimum(z, -z)))
            if blocks[c // n_heads][1]:
                softplus = jnp.where(causal, softplus, 0.0)
            return z - softplus, softplus.astype(BF16), softplus[:, 0:1]

        def weighted_values(c, log_beta, rest, first_col):
            n, first = c % n_heads, c < n_heads and blocks[0][1]
            arg = log_beta + rest
            if not first:
                arg = arg + carry_ref[n]
            w = jnp.exp(arg)
            if blocks[c // n_heads][1]:
                w = jnp.where(causal, w, 0.0)
            v = v_ref[0, pl.ds(starts[c // n_heads], blk), cols(c)]
            vh = jnp.where(head_lanes[c % 2], v, zero_bf)
            block_sum = rest[:, 0:1] - first_col
            carry = block_sum if first else carry_ref[n] + block_sum
            carry_ref[n] = carry
            return _dot(w.astype(BF16), vh), carry

        z, terms, rest, pv, bound = {}, {}, {}, {}, None
        for step in range(n_chains + 3):
            c4, c3, c2, c1 = step - 3, step - 2, step - 1, step
            if 0 <= c3 < n_chains:
                rest[c3] = _dot(terms[c3][1], upper2)
            if 0 <= c1 < n_chains:
                z[c1] = scores(c1)
            if 0 <= c4 < n_chains:
                pv[c4], carry = weighted_values(c4, terms[c4][0], rest.pop(c4), terms[c4][2])
                del terms[c4]
                if c4 >= n_chains - n_heads:
                    bound = carry if bound is None else jnp.maximum(bound, carry)
                if c4 % 2 == 1:
                    both = pv.pop(c4 - 1) + pv.pop(c4)
                    first = c4 < n_heads and blocks[0][1]
                    o_ref[0, :, cols(c4)] = both if first else o_ref[0, :, cols(c4)] + both
            if 0 <= c2 < n_chains:
                terms[c2] = log_terms(c2, z.pop(c2))
        return (jnp.max(bound) > EXP_ZERO_BELOW).astype(jnp.int32)

    @pl.when(qi == 0)
    def _():
        add_blocks([(qi, True)])

    @pl.when(qi > 0)
    def _():
        go = add_blocks([(qi, True), (qi - 1, False)])

        def cond(state):
            return jnp.logical_and(state[0] >= 0, state[1] > 0)

        def body(state):
            return state[0] - 1, add_blocks([(state[0], False)])

        lax.while_loop(cond, body, (qi - 2, go))


def _attn_and_cast_kernel(*refs, n_cast):
    q_ref, k_ref, v_ref = refs[:3]
    srcs = refs[3:3 + n_cast]
    o_ref = refs[3 + n_cast]
    dsts = refs[4 + n_cast:4 + 2 * n_cast]
    carry_ref = refs[4 + 2 * n_cast]
    for src, dst in zip(srcs, dsts):
        dst[...] = src[...].astype(BF16)
    _attn_kernel(q_ref, k_ref, v_ref, o_ref, carry_ref)


BF16_ROWS = 16


def _attention(qkv3, cast_weights=()):
    b, s, _ = qkv3.shape
    blk = min(ATTN_BLOCK, s)
    n_q = s // blk
    steps = b * n_q
    cast_2d, in_cast, out_cast, out_shapes = [], [], [], []
    for w, index in cast_weights:
        cols = w.shape[-1]
        layer_rows = math.prod(w.shape[1:-1])
        hold = 1
        while steps % hold or layer_rows % (steps // hold) or (layer_rows // (steps // hold)) % BF16_ROWS:
            hold += 1
        n_blocks = steps // hold
        rows = layer_rows // n_blocks

        def in_map(bi, qi, index=index, hold=hold, n_blocks=n_blocks):
            return index * n_blocks + (bi * n_q + qi) // hold, 0

        def out_map(bi, qi, hold=hold):
            return (bi * n_q + qi) // hold, 0

        cast_2d.append(w.reshape(-1, cols))
        in_cast.append(pl.BlockSpec((rows, cols), in_map))
        out_cast.append(pl.BlockSpec((rows, cols), out_map))
        out_shapes.append(jax.ShapeDtypeStruct((layer_rows, cols), BF16))
    outs = pl.pallas_call(
        functools.partial(_attn_and_cast_kernel, n_cast=len(cast_2d)),
        grid=(b, n_q),
        in_specs=[pl.BlockSpec((1, blk, ATTN_W), lambda bi, qi: (bi, qi, 0)),
                  pl.BlockSpec((1, s, ATTN_W), lambda bi, qi: (bi, 0, 1)),
                  pl.BlockSpec((1, s, ATTN_W), lambda bi, qi: (bi, 0, 2))] + in_cast,
        out_specs=[pl.BlockSpec((1, blk, ATTN_W), lambda bi, qi: (bi, qi, 0))] + out_cast,
        out_shape=[jax.ShapeDtypeStruct((b, s, ATTN_W), F32)] + out_shapes,
        scratch_shapes=[pltpu.VMEM((N_HEADS, blk, 1), F32)],
        compiler_params=_cparams(2),
        name="sb_attention",
    )(qkv3, qkv3, qkv3, *cast_2d)
    return outs[0], [o.reshape(w.shape[1:]) for o, (w, _) in zip(outs[1:], cast_weights)]


def _outproj_rows(x_ref, oa_ref, obc_ref, gn_ref, wo_ref):
    oa = _rms(oa_ref[...], gn_ref[:, 0:ATTN_W]).astype(BF16)
    o = jnp.concatenate([oa, obc_ref[...]], axis=1)
    return x_ref[...] + _dot(o, wo_ref[...])


def _swiglu_rows(h_bf16, wg_ref, wu_ref, wd_ref, acc):
    f0 = 0
    for fc in FF_CHUNKS:
        gate = _dot(h_bf16, wg_ref[:, f0:f0 + fc])
        up = _dot(h_bf16, wu_ref[:, f0:f0 + fc])
        act = (gate * _sigmoid(gate) * up).astype(BF16)
        acc = acc + _dot(act, wd_ref[f0:f0 + fc, :])
        f0 += fc
    return acc


def _outproj_dense_kernel(x_ref, oa_ref, obc_ref, gn_ref, wo_ref, g_ref, wg_ref, wu_ref, wd_ref,
                          fg_ref, out_ref, *, final_norm):
    x = _outproj_rows(x_ref, oa_ref, obc_ref, gn_ref, wo_ref)
    h = _rms(x, g_ref[...]).astype(BF16)
    y = _swiglu_rows(h, wg_ref, wu_ref, wd_ref, x)
    if final_norm:
        y = _rms(y, fg_ref[...])
    out_ref[...] = y


def _outproj_dense(x2, oa2, obc, gn_g, wo, g, wg, wu, wd, final_g, final_norm):
    t, d = x2.shape
    f = wg.shape[1]
    assert sum(FF_CHUNKS) == f
    tm = min(TOKEN_TILE, t)
    row = lambda a: a.reshape(1, -1)
    return pl.pallas_call(
        functools.partial(_outproj_dense_kernel, final_norm=final_norm),
        grid=(t // tm,),
        in_specs=[pl.BlockSpec((tm, d), lambda i: (i, 0)),
                  pl.BlockSpec((tm, ATTN_W), lambda i: (i, 0)),
                  pl.BlockSpec((tm, POOL_W + CONV_W), lambda i: (i, 0)),
                  _const_spec((1, d)),
                  _const_spec((d, d)),
                  _const_spec((1, d)),
                  _const_spec((d, f)),
                  _const_spec((d, f)),
                  _const_spec((f, d)),
                  _const_spec((1, d))],
        out_specs=pl.BlockSpec((tm, d), lambda i: (i, 0)),
        out_shape=jax.ShapeDtypeStruct((t, d), F32),
        compiler_params=_cparams(1),
        name="outproj_dense_ffn",
    )(x2, oa2, obc, row(gn_g), wo, row(g), wg, wu, wd, row(final_g))


META_E0, META_E1, META_G0, META_G1, META_R0, META_R1 = range(6)


def _split3(a):
    p0 = a.astype(BF16)
    r1 = a - p0.astype(F32)
    p1 = r1.astype(BF16)
    p2 = (r1 - p1.astype(F32)).astype(BF16)
    return p0, p1, p2


def _outproj_router_kernel(x_ref, oa_ref, obc_ref, gn_ref, wo_ref, g_ref, wrt_ref,
                           x1_ref, meta_ref, counts_ref, run_ref):
    tm = x_ref.shape[0]
    i = pl.program_id(0)
    nt_dims = (((1,), (1,)), ((), ()))

    @pl.when(i == 0)
    def _():
        run_ref[...] = jnp.zeros_like(run_ref)

    x1 = _outproj_rows(x_ref, oa_ref, obc_ref, gn_ref, wo_ref)
    x1_ref[...] = x1
    h = _rms(x1, g_ref[...])
    hs = _split3(h)
    ws = [p.astype(F32) for p in _split3(wrt_ref[...])]
    logits = jnp.zeros((N_EXPERTS, tm), F32)
    for a in (2, 1, 0):
        stacked = jnp.concatenate(ws[:3 - a], axis=0).astype(BF16)
        part = lax.dot_general(stacked, hs[a], nt_dims, preferred_element_type=F32)
        for b in range(3 - a):
            logits = logits + part[b * N_EXPERTS:(b + 1) * N_EXPERTS, :]

    eidx = lax.broadcasted_iota(jnp.int32, (N_EXPERTS, tm), 0).astype(F32)
    neg = jnp.float32(-jnp.inf)
    v0 = jnp.max(logits, axis=0, keepdims=True)
    e0 = jnp.min(jnp.where(logits == v0, eidx, float(N_EXPERTS)), axis=0, keepdims=True)
    masked = jnp.where(eidx == e0, neg, logits)
    v1 = jnp.max(masked, axis=0, keepdims=True)
    e1 = jnp.min(jnp.where(masked == v1, eidx, float(N_EXPERTS)), axis=0, keepdims=True)
    ex = jnp.exp(v1 - v0)
    g0 = 1.0 / (1.0 + ex)
    g1 = ex / (1.0 + ex)

    sel0 = eidx == e0
    sel1 = eidx == e1
    onehot = jnp.where(jnp.logical_or(sel0, sel1), 1.0, 0.0)
    rr = lax.broadcasted_iota(jnp.int32, (tm, tm), 0)
    cc = lax.broadcasted_iota(jnp.int32, (tm, tm), 1)
    earlier = jnp.where(rr < cc, 1.0, 0.0).astype(BF16)
    run = run_ref[:, 0:1]
    rank_all = _dot(onehot.astype(BF16), earlier) + run
    r0 = jnp.sum(jnp.where(sel0, rank_all, 0.0), axis=0, keepdims=True)
    r1 = jnp.sum(jnp.where(sel1, rank_all, 0.0), axis=0, keepdims=True)
    new_run = run + jnp.sum(onehot, axis=1, keepdims=True)
    run_ref[...] = jnp.broadcast_to(new_run, run_ref.shape)
    counts_ref[...] = jnp.broadcast_to(new_run, counts_ref.shape)

    zero_row = jnp.zeros((1, tm), F32)
    meta_ref[...] = jnp.concatenate([e0, e1, g0, g1, r0, r1, zero_row, zero_row], axis=0)


def _outproj_router(x2, oa2, obc, gn_g, wo, g, wr_t):
    t, d = x2.shape
    tm = min(TOKEN_TILE, t)
    row = lambda a: a.reshape(1, -1)
    return pl.pallas_call(
        _outproj_router_kernel,
        grid=(t // tm,),
        in_specs=[pl.BlockSpec((tm, d), lambda i: (i, 0)),
                  pl.BlockSpec((tm, ATTN_W), lambda i: (i, 0)),
                  pl.BlockSpec((tm, POOL_W + CONV_W), lambda i: (i, 0)),
                  _const_spec((1, d)),
                  _const_spec((d, d)),
                  _const_spec((1, d)),
                  _const_spec((N_EXPERTS, d))],
        out_specs=[pl.BlockSpec((tm, d), lambda i: (i, 0)),
                   pl.BlockSpec((SUBLANES, tm), lambda i: (0, i)),
                   pl.BlockSpec((N_EXPERTS, LANES), lambda i: (0, 0))],
        out_shape=[jax.ShapeDtypeStruct((t, d), F32),
                   jax.ShapeDtypeStruct((SUBLANES, t), F32),
                   jax.ShapeDtypeStruct((N_EXPERTS, LANES), F32)],
        scratch_shapes=[pltpu.VMEM((N_EXPERTS, LANES), F32)],
        compiler_params=_cparams(1),
        name="outproj_router",
    )(x2, oa2, obc, row(gn_g), wo, row(g), wr_t)


ROWS_PER_ISSUE = 8


def _row_view(ref, row):
    return ref.at[pl.ds(pl.multiple_of(row * SUBLANES, SUBLANES), SUBLANES), :]


def _dispatch_kernel(pos_ref, pad_start_ref, pad_on_ref, x_ref, g_ref, xs_ref, rows, zeros, sems, zsem):
    tm = x_ref.shape[0]
    i = pl.program_id(0)
    n = pl.num_programs(0)
    slot = lax.rem(i, 2)
    tile_rows = tm * SUBLANES

    def drain(s):
        for _ in range(TOP_K):
            pltpu.make_async_copy(rows.at[s], xs_ref.at[pl.ds(0, tile_rows), :], sems.at[s]).wait()

    @pl.when(i == 0)
    def _():
        zeros[...] = jnp.zeros_like(zeros)
        zrows = zeros.shape[0]
        for e in range(pad_start_ref.shape[0]):
            @pl.when(pad_on_ref[e] > 0)
            def _():
                start = pl.multiple_of(pad_start_ref[e] * SUBLANES, SUBLANES)
                pltpu.make_async_copy(zeros, xs_ref.at[pl.ds(start, zrows), :], zsem).start()
        for e in range(pad_start_ref.shape[0]):
            @pl.when(pad_on_ref[e] > 0)
            def _():
                pltpu.make_async_copy(zeros, xs_ref.at[pl.ds(0, zrows), :], zsem).wait()

    @pl.when(i >= 2)
    def _():
        drain(slot)

    h = _rms(x_ref[...], g_ref[...])
    buf = rows.at[slot]
    for j in range(h.shape[1] // LANES):
        buf[pl.ds(j, tm, stride=SUBLANES), :] = h[:, j * LANES:(j + 1) * LANES]

    def issue(c, _):
        for u in range(ROWS_PER_ISSUE):
            r = c * ROWS_PER_ISSUE + u
            for k in range(TOP_K):
                dst = pos_ref[0, 0, k * tm + r]
                pltpu.make_async_copy(_row_view(buf, r), _row_view(xs_ref, dst),
                                      sems.at[slot]).start(priority=k)
        return 0

    lax.fori_loop(0, tm // ROWS_PER_ISSUE, issue, 0)

    @pl.when(i == n - 1)
    def _():
        @pl.when(n >= 2)
        def _():
            drain(1 - slot)
        drain(slot)


def _tile_positions(pos_t, tm):
    n_tiles = pos_t.shape[1] // tm
    return pos_t.reshape(TOP_K, n_tiles, tm).transpose(1, 0, 2).reshape(n_tiles, 1, TOP_K * tm)


def _dispatch(x2, g, pos_t, pad_start, pad_on, n_rows):
    t, d = x2.shape
    tm = min(TOKEN_TILE, t)
    n_tiles = t // tm
    pos3 = _tile_positions(pos_t, tm)
    grid_spec = pltpu.PrefetchScalarGridSpec(
        num_scalar_prefetch=0,
        grid=(n_tiles,),
        in_specs=[pl.BlockSpec((1, 1, tm * TOP_K), lambda i: (i, 0, 0), memory_space=pltpu.SMEM),
                  pl.BlockSpec(memory_space=pltpu.SMEM),
                  pl.BlockSpec(memory_space=pltpu.SMEM),
                  pl.BlockSpec((tm, d), lambda i: (i, 0)),
                  _const_spec((1, d))],
        out_specs=pl.BlockSpec(memory_space=pl.ANY),
        scratch_shapes=[pltpu.VMEM((2, tm * SUBLANES, LANES), F32),
                        pltpu.VMEM((GROUP_TILE * SUBLANES, LANES), F32),
                        pltpu.SemaphoreType.DMA((2,)),
                        pltpu.SemaphoreType.DMA(())],
    )
    return pl.pallas_call(
        _dispatch_kernel,
        grid_spec=grid_spec,
        out_shape=jax.ShapeDtypeStruct((n_rows * SUBLANES, LANES), F32),
        compiler_params=_cparams(1),
        name="moe_dispatch",
    )(pos3, pad_start, pad_on, x2, g.reshape(1, d))


def _group_ffn_kernel(te_ref, last_ref, xs_ref, wg_ref, wu_ref, wd_ref, ys_ref):
    i = pl.program_id(0)
    tg = xs_ref.shape[0] // SUBLANES
    d = wg_ref.shape[1]

    @pl.when(i <= last_ref[0])
    def _():
        x = jnp.concatenate([xs_ref[pl.ds(j, tg, stride=SUBLANES), :] for j in range(d // LANES)],
                            axis=1).astype(BF16)
        y = _swiglu_rows(x, wg_ref.at[0], wu_ref.at[0], wd_ref.at[0], jnp.zeros((tg, d), F32))
        for j in range(d // LANES):
            ys_ref[pl.ds(j, tg, stride=SUBLANES), :] = y[:, j * LANES:(j + 1) * LANES]

    @pl.when(i > last_ref[0])
    def _():
        ys_ref[...] = jnp.zeros_like(ys_ref)


def _group_ffn(xs, tile_expert, last_tile, wg, wu, wd):
    n_tiles = tile_expert.shape[0]
    _, d, f = wg.shape
    blk = GROUP_TILE * SUBLANES
    grid_spec = pltpu.PrefetchScalarGridSpec(
        num_scalar_prefetch=2,
        grid=(n_tiles,),
        in_specs=[pl.BlockSpec((blk, LANES), lambda i, te, last: (jnp.minimum(i, last[0]), 0)),
                  pl.BlockSpec((1, d, f), lambda i, te, last: (te[i], 0, 0)),
                  pl.BlockSpec((1, d, f), lambda i, te, last: (te[i], 0, 0)),
                  pl.BlockSpec((1, f, d), lambda i, te, last: (te[i], 0, 0))],
        out_specs=pl.BlockSpec((blk, LANES), lambda i, te, last: (i, 0)),
    )
    return pl.pallas_call(
        _group_ffn_kernel,
        grid_spec=grid_spec,
        out_shape=jax.ShapeDtypeStruct(xs.shape, F32),
        compiler_params=_cparams(1),
        name="moe_group_ffn",
    )(tile_expert, last_tile, xs, wg, wu, wd)


def _gather_row(p_ref, ys_ref, bufs, sems, s, tm, r, k):
    src = p_ref[0, 0, k * tm + r]
    pltpu.make_async_copy(_row_view(ys_ref, src), _row_view(bufs.at[s, k], r),
                          sems.at[s]).start(priority=k)


def _gather_tile(p_ref, ys_ref, bufs, sems, s, tm):
    def issue(c, _):
        for u in range(ROWS_PER_ISSUE):
            for k in range(TOP_K):
                _gather_row(p_ref, ys_ref, bufs, sems, s, tm, c * ROWS_PER_ISSUE + u, k)
        return 0
    lax.fori_loop(0, tm // ROWS_PER_ISSUE, issue, 0)


def _wait_tile(ys_ref, bufs, sems, s, tm):
    for k in range(TOP_K):
        pltpu.make_async_copy(ys_ref.at[pl.ds(0, tm * SUBLANES), :], bufs.at[s, k], sems.at[s]).wait()


def _combine_rows(pos_ref, pos_next_ref, x_ref, meta_ref, ys_ref, bufs, sems, spread=0):
    tm = x_ref.shape[0]
    i = pl.program_id(0)
    n = pl.num_programs(0)
    slot = lax.rem(i, 2)

    @pl.when(i == 0)
    def _():
        _gather_tile(pos_ref, ys_ref, bufs, sems, 0, tm)

    if not spread:
        @pl.when(i + 1 < n)
        def _():
            _gather_tile(pos_next_ref, ys_ref, bufs, sems, 1 - slot, tm)

    _wait_tile(ys_ref, bufs, sems, slot, tm)

    batches = iter(range(spread))

    def start_some():
        b = next(batches, None)
        if b is not None:
            per = tm // spread
            for r in range(b * per, (b + 1) * per):
                for k in range(TOP_K):
                    _gather_row(pos_next_ref, ys_ref, bufs, sems, 1 - slot, tm, r, k)

    meta = meta_ref[...]
    g0 = meta[:, META_G0:META_G0 + 1]
    g1 = meta[:, META_G1:META_G1 + 1]
    x = x_ref[...]
    cols = []
    for j in range(x.shape[1] // LANES):
        y0 = bufs[slot, 0, pl.ds(j, tm, stride=SUBLANES), :]
        y1 = bufs[slot, 1, pl.ds(j, tm, stride=SUBLANES), :]
        cols.append(x[:, j * LANES:(j + 1) * LANES] + (g0 * y0 + g1 * y1))
    y = jnp.concatenate(cols, axis=1)
    return (y, start_some) if spread else y


def _combine_kernel(pos_ref, pos_next_ref, x_ref, meta_ref, fg_ref, ys_ref, out_ref, bufs, sems):
    y = _combine_rows(pos_ref, pos_next_ref, x_ref, meta_ref, ys_ref, bufs, sems)
    out_ref[...] = _rms(y, fg_ref[...])


GATHER_SPREAD = 16


def _combine_inproj_mix_kernel(pos_ref, pos_next_ref, x_ref, meta_ref, ys_ref, *refs, tiles_per_seq):
    n_mix_in = 9
    x_out_ref, qkv_ref, obc_ref = refs[n_mix_in:n_mix_in + 3]
    bufs, sems = refs[n_mix_in + 3:n_mix_in + 5]
    mix_scratch = refs[n_mix_in + 5:]
    tm = x_ref.shape[0]
    i = pl.program_id(0)
    x, start_some = _combine_rows(pos_ref, pos_next_ref, x_ref, meta_ref, ys_ref, bufs, sems,
                                  spread=GATHER_SPREAD)
    x_out_ref[...] = x
    _inproj_mix_rows(x, *refs[:n_mix_in], qkv_ref, obc_ref, *mix_scratch, tiles_per_seq=tiles_per_seq,
                     between_stages=start_some)
    for _ in range(GATHER_SPREAD):
        start_some()

    @pl.when(i == pl.num_programs(0) - 1)
    def _():
        _wait_tile(ys_ref, bufs, sems, 1 - lax.rem(i, 2), tm)


def _combine_specs(t, d, tm):
    n_tiles = t // tm
    in_specs = [pl.BlockSpec((1, 1, tm * TOP_K), lambda i: (i, 0, 0), memory_space=pltpu.SMEM),
                pl.BlockSpec((1, 1, tm * TOP_K), lambda i: (jnp.minimum(i + 1, n_tiles - 1), 0, 0),
                             memory_space=pltpu.SMEM),
                pl.BlockSpec((tm, d), lambda i: (i, 0)),
                pl.BlockSpec((tm, SUBLANES), lambda i: (i, 0))]
    scratch = [pltpu.VMEM((2, TOP_K, tm * SUBLANES, LANES), F32), pltpu.SemaphoreType.DMA((2,))]
    return in_specs, scratch


def _combine(x2, meta, pos_t, ys, final_g):
    t, d = x2.shape
    tm = min(GROUP_TILE, t)
    pos3 = _tile_positions(pos_t, tm)
    in_specs, scratch = _combine_specs(t, d, tm)
    return pl.pallas_call(
        _combine_kernel,
        grid=(t // tm,),
        in_specs=in_specs + [_const_spec((1, d)), pl.BlockSpec(memory_space=pl.ANY)],
        out_specs=pl.BlockSpec((tm, d), lambda i: (i, 0)),
        out_shape=jax.ShapeDtypeStruct((t, d), F32),
        scratch_shapes=scratch,
        compiler_params=_cparams(1),
        name="moe_combine",
    )(pos3, pos3, x2, meta, final_g.reshape(1, d), ys)


def _combine_inproj_mix(x2, meta, pos_t, ys, seq_len, g, w_bf16, wp_bd, pool_scale, conv_w, conv_b,
                        ln_g, ln_b, gn_g):
    t, d = x2.shape
    tm = min(TOKEN_TILE, seq_len)
    pos3 = _tile_positions(pos_t, tm)
    in_specs, scratch = _combine_specs(t, d, tm)
    mix_in, mix_out, mix_shapes, mix_scratch = _inproj_mix_specs(t, d, w_bf16.shape[1], tm)
    row = lambda a: a.reshape(1, -1)
    return pl.pallas_call(
        functools.partial(_combine_inproj_mix_kernel, tiles_per_seq=seq_len // tm),
        grid=(t // tm,),
        in_specs=in_specs + [pl.BlockSpec(memory_space=pl.ANY)] + mix_in,
        out_specs=[pl.BlockSpec((tm, d), lambda i: (i, 0))] + mix_out,
        out_shape=[jax.ShapeDtypeStruct((t, d), F32)] + mix_shapes,
        scratch_shapes=scratch + mix_scratch,
        compiler_params=_cparams(1),
        name="combine_inproj_mix",
    )(pos3, pos3, x2, meta, ys, row(g), w_bf16, wp_bd, row(pool_scale), conv_w, row(conv_b),
      row(ln_g), row(ln_b), row(gn_g))


def _moe_expert_rows(x2, meta_t, counts, g, wg, wu, wd):
    t, d = x2.shape
    meta = meta_t.T

    cnt = counts[:, 0].astype(jnp.int32)
    padded = ((cnt + GROUP_TILE - 1) // GROUP_TILE) * GROUP_TILE
    ends = jnp.cumsum(padded)
    offs = ends - padded
    n_tiles = (t * TOP_K) // GROUP_TILE + N_EXPERTS
    n_rows = n_tiles * GROUP_TILE
    eid_t = meta_t[META_E0:META_E1 + 1].astype(jnp.int32)
    rank_t = meta_t[META_R0:META_R1 + 1].astype(jnp.int32)
    pos_t = rank_t
    for e in range(N_EXPERTS):
        pos_t = pos_t + jnp.where(eid_t == e, offs[e], 0)
    tail_tiles = ends[-1] + jnp.arange(N_EXPERTS, dtype=jnp.int32) * GROUP_TILE
    pad_start = jnp.concatenate([ends - GROUP_TILE, tail_tiles])
    pad_on = jnp.concatenate([cnt > 0, tail_tiles < n_rows]).astype(jnp.int32)
    pad_start = jnp.clip(pad_start, 0, n_rows - GROUP_TILE).astype(jnp.int32)
    tile_start = jnp.arange(n_tiles, dtype=jnp.int32) * GROUP_TILE
    last_tile = jnp.maximum(ends[-1] // GROUP_TILE - 1, 0).astype(jnp.int32).reshape(1)
    tile_expert = jnp.sum(tile_start[:, None] >= ends[None, :], axis=1).astype(jnp.int32)
    tile_expert = jnp.minimum(tile_expert, tile_expert[last_tile[0]])

    xs = _dispatch(x2, g, pos_t, pad_start, pad_on, n_rows)
    ys = _group_ffn(xs, tile_expert, last_tile, wg, wu, wd)
    return meta, pos_t, ys


def kernel(x, attn_norm_g, w_in, pool_w, pool_scale, conv_w, conv_b, conv_ln_g, conv_ln_b,
           group_norm_g, w_out, ffn_norm_g, dense_w_gate, dense_w_up, dense_w_down,
           router_w, moe_w_gate, moe_w_up, moe_w_down, final_norm_g):
    b, s, d = x.shape
    depth = w_in.shape[0]
    t = b * s
    def mix_params(l, w_in_bf16):
        wp_bd = jax.scipy.linalg.block_diag(*[pool_w[l, gi] for gi in range(pool_w.shape[1])]).astype(BF16)
        return (attn_norm_g[l], w_in_bf16, wp_bd, pool_scale[l], conv_w[l], conv_b[l], conv_ln_g[l],
                conv_ln_b[l], group_norm_g[l])

    x2 = x.reshape(t, d)
    qkv, obc = _inproj_mix(x2, s, *mix_params(0, w_in[0].astype(BF16)))
    for l in range(depth):
        last = l == depth - 1
        i = l // 2
        ffn_f32 = ((dense_w_gate, dense_w_up, dense_w_down) if l % 2 == 0
                   else (moe_w_gate, moe_w_up, moe_w_down))
        jobs = [(w, i) for w in ffn_f32] + [(w_out, l)] + ([] if last else [(w_in, l + 1)])
        o_a, casts = _attention(qkv.reshape(b, s, -1), jobs)
        ffn_bf16, w_out_bf16 = casts[:3], casts[3]
        next_mix = None if last else mix_params(l + 1, casts[4])
        oa2 = o_a.reshape(t, -1)
        if l % 2 == 0:
            x2 = _outproj_dense(x2, oa2, obc, group_norm_g[l], w_out_bf16, ffn_norm_g[l], *ffn_bf16,
                                final_norm_g, last)
            if not last:
                qkv, obc = _inproj_mix(x2, s, *next_mix)
        else:
            x1, meta_t, counts = _outproj_router(x2, oa2, obc, group_norm_g[l], w_out_bf16,
                                                 ffn_norm_g[l], router_w[i].T)
            meta, pos_t, ys = _moe_expert_rows(x1, meta_t, counts, ffn_norm_g[l], *ffn_bf16)
            if last:
                x2 = _combine(x1, meta, pos_t, ys, final_norm_g)
            else:
                x2, qkv, obc = _combine_inproj_mix(x1, meta, pos_t, ys, s, *next_mix)
    return x2.reshape(b, s, d)
```

```python
import functools
import math

import jax
import jax.numpy as jnp
from jax import lax
from jax.experimental import pallas as pl
from jax.experimental.pallas import tpu as pltpu

F32 = jnp.float32
BF16 = jnp.bfloat16

LANES = 128
SUBLANES = 8

N_HEADS = 8
HEAD_DIM = 64
ATTN_W = N_HEADS * HEAD_DIM
POOL_W = 256
CONV_W = 256
POOL_WINDOWS = (2, 4, 8, 16)
CONV_K = 31
HALO = 32
N_EXPERTS = 8
TOP_K = 2
RMS_EPS = 1e-6
LN_EPS = 1e-5

ATTN_BLOCK = 256
Q_BLOCKS_PER_STEP = 2
EXP_ZERO_BELOW = -105.0

TOKEN_TILE = 512
ROW_CHUNK = 64
GROUP_TILE = 256
FF_CHUNKS = (1024, 1024, 768)
QKV_CHUNK = 256

VMEM_LIMIT = 56 * 1024 * 1024


def _cparams(n_axes, vmem=VMEM_LIMIT):
    return pltpu.CompilerParams(dimension_semantics=("arbitrary",) * n_axes, vmem_limit_bytes=vmem)


def _rms(x, g):
    return x * lax.rsqrt(jnp.mean(x * x, axis=-1, keepdims=True) + RMS_EPS) * g


def _dot(a, b):
    return jnp.dot(a, b, preferred_element_type=F32)


def _const_spec(shape):
    zeros = (0,) * len(shape)
    return pl.BlockSpec(shape, lambda *_: zeros, pipeline_mode=pl.Buffered(1))


def _sigmoid(a):
    return 1.0 / (1.0 + jnp.exp(-a))


def _inproj_mix_kernel(x_ref, *refs, tiles_per_seq):
    _inproj_mix_rows(x_ref[...], *refs, tiles_per_seq=tiles_per_seq)


def _inproj_mix_rows(x, g_ref, w_ref, wp_ref, ps_ref, cw_ref, cb_ref, lg_ref, lb_ref, gn_ref,
                     qkv_ref, obc_ref, pool_ext, conv_ext, conv_shift, *, tiles_per_seq,
                     between_stages=lambda: None):
    tm = x.shape[0]
    i = pl.program_id(0)
    in_sequence = lax.rem(i, tiles_per_seq) > 0
    ext_rows = HALO + tm

    @pl.when(i == 0)
    def _():
        pool_ext[...] = jnp.zeros_like(pool_ext)
        conv_ext[...] = jnp.zeros_like(conv_ext)

    h = _rms(x, g_ref[...]).astype(BF16)
    nq = qkv_ref.shape[-1]
    rest = _dot(h, w_ref[:, nq:])
    pool_ext[0:HALO, :] = jnp.where(in_sequence, pool_ext[tm:ext_rows, :], 0.0)
    conv_ext[0:HALO, :] = jnp.where(in_sequence, conv_ext[tm:ext_rows, :], 0.0)
    pool_ext[HALO:ext_rows, :] = rest[:, 0:POOL_W]
    conv_ext[HALO:ext_rows, :] = (rest[:, POOL_W:POOL_W + CONV_W]
                                  * _sigmoid(rest[:, POOL_W + CONV_W:]))

    qkv_chunks = iter(range(nq // QKV_CHUNK))

    def qkv_part():
        c = next(qkv_chunks, None)
        if c is not None:
            cols = slice(c * QKV_CHUNK, (c + 1) * QKV_CHUNK)
            qkv_ref[:, cols] = _dot(h, w_ref[:, cols]).astype(BF16)

    lane = lax.broadcasted_iota(jnp.int32, (1, LANES), 1)
    low_half = lane < (LANES // 2)
    rc = min(ROW_CHUNK, tm)
    seq_pos = lax.rem(i, tiles_per_seq) * tm
    pooled_rows = []
    for r0 in range(0, tm, rc):
        pos = seq_pos + r0 + lax.broadcasted_iota(jnp.int32, (rc, 1), 0)
        halves = []
        for c, (w_lo, w_hi) in enumerate(((POOL_WINDOWS[0], POOL_WINDOWS[1]),
                                          (POOL_WINDOWS[2], POOL_WINDOWS[3]))):
            cols = slice(c * LANES, (c + 1) * LANES)
            base = HALO + r0
            ident = pool_ext[base:base + rc, cols]
            s_lo = ident
            for j in range(1, w_lo):
                s_lo = s_lo + pool_ext[base - j:base - j + rc, cols]
            s_hi = s_lo
            for j in range(w_lo, w_hi):
                s_hi = s_hi + pool_ext[base - j:base - j + rc, cols]
            total = jnp.where(low_half, s_lo, s_hi)
            win = jnp.where(low_half, w_lo, w_hi)
            count = jnp.minimum(pos + 1, win).astype(F32)
            halves.append(total / count - ident)
        pooled_rows.append(jnp.concatenate(halves, axis=1))
        between_stages()
        if r0 % (2 * rc) == 0:
            qkv_part()
    pooled = jnp.concatenate(pooled_rows, axis=0).astype(BF16)
    o_b = _dot(pooled, wp_ref[...]) * ps_ref[...]

    for s in range(1, SUBLANES):
        conv_shift[s - 1] = conv_ext[s:s + ext_rows, :]
    conv_rows = []
    for r0 in range(0, tm, rc):
        qkv_part()
        part = jnp.zeros((rc, CONV_W), F32) + cb_ref[...]
        for j in range(CONV_K):
            off = HALO - (CONV_K - 1) + j + r0
            s, aligned = off % SUBLANES, off - off % SUBLANES
            window = (conv_ext[aligned:aligned + rc, :] if s == 0
                      else conv_shift[s - 1, aligned:aligned + rc, :])
            part = part + cw_ref[j:j + 1, :] * window
        conv_rows.append(part)
        between_stages()
    for _ in qkv_chunks:
        raise AssertionError("tile too small to place every q/k/v matmul chunk")
    conv = jnp.concatenate(conv_rows, axis=0)
    mu = jnp.mean(conv, axis=-1, keepdims=True)
    cen = conv - mu
    var = jnp.mean(cen * cen, axis=-1, keepdims=True)
    ln = cen * lax.rsqrt(var + LN_EPS) * lg_ref[...] + lb_ref[...]
    o_c = ln * _sigmoid(ln)

    gn = gn_ref[...]
    obc_ref[...] = jnp.concatenate([_rms(o_b, gn[:, ATTN_W:ATTN_W + POOL_W]),
                                    _rms(o_c, gn[:, ATTN_W + POOL_W:])], axis=1).astype(BF16)


def _inproj_mix_specs(t, d, n, tm):
    assert HALO >= CONV_K - 1 and HALO >= max(POOL_WINDOWS) - 1 and tm >= HALO
    nq = 3 * ATTN_W
    in_specs = [_const_spec((1, d)),
                _const_spec((d, n)),
                _const_spec((POOL_W, POOL_W)),
                _const_spec((1, POOL_W)),
                _const_spec((CONV_K, CONV_W)),
                _const_spec((1, CONV_W)),
                _const_spec((1, CONV_W)),
                _const_spec((1, CONV_W)),
                _const_spec((1, d))]
    out_specs = [pl.BlockSpec((tm, nq), lambda i: (i, 0)),
                 pl.BlockSpec((tm, POOL_W + CONV_W), lambda i: (i, 0))]
    out_shapes = [jax.ShapeDtypeStruct((t, nq), BF16),
                  jax.ShapeDtypeStruct((t, POOL_W + CONV_W), BF16)]
    scratch = [pltpu.VMEM((HALO + tm, POOL_W), F32),
               pltpu.VMEM((HALO + tm + SUBLANES, CONV_W), F32),
               pltpu.VMEM((SUBLANES - 1, HALO + tm, CONV_W), F32)]
    return in_specs, out_specs, out_shapes, scratch


def _inproj_mix(x2, seq_len, g, w_bf16, wp_bd, pool_scale, conv_w, conv_b, ln_g, ln_b, gn_g):
    t, d = x2.shape
    tm = min(TOKEN_TILE, seq_len)
    in_specs, out_specs, out_shapes, scratch = _inproj_mix_specs(t, d, w_bf16.shape[1], tm)
    row = lambda a: a.reshape(1, -1)
    return pl.pallas_call(
        functools.partial(_inproj_mix_kernel, tiles_per_seq=seq_len // tm),
        grid=(t // tm,),
        in_specs=[pl.BlockSpec((tm, d), lambda i: (i, 0))] + in_specs,
        out_specs=out_specs,
        out_shape=out_shapes,
        scratch_shapes=scratch,
        compiler_params=_cparams(1),
        name="inproj_mix",
    )(x2, row(g), w_bf16, wp_bd, row(pool_scale), conv_w, row(conv_b), row(ln_g), row(ln_b), row(gn_g))


def _attn_kernel(q_ref, k_ref, v_ref, o_ref, carry_ref):
    blk = q_ref.shape[1] // Q_BLOCKS_PER_STEP
    n_pairs = q_ref.shape[2] // LANES
    qi = pl.program_id(1)
    scale = jnp.asarray(1.0 / math.sqrt(HEAD_DIM), BF16)

    lane = lax.broadcasted_iota(jnp.int32, (1, LANES), 1)
    head_lanes = (lane < HEAD_DIM, lane >= HEAD_DIM)
    zero_bf = jnp.zeros((), BF16)

    jj = lax.broadcasted_iota(jnp.int32, (blk, blk), 0)
    ss = lax.broadcasted_iota(jnp.int32, (blk, blk), 1)
    upper2 = jnp.where(jj > ss, -1.0, 0.0).astype(BF16)

    n_heads = 2 * n_pairs

    def add_blocks(blocks):
        starts = [pl.multiple_of(j * blk, blk) for _, j, _ in blocks]
        if any(diagonal for _, _, diagonal in blocks):
            r_idx = lax.broadcasted_iota(jnp.int32, (blk, blk), 0)
            c_idx = lax.broadcasted_iota(jnp.int32, (blk, blk), 1)
            causal = c_idx < r_idx
        n_chains = n_heads * len(blocks)
        last_block_of = {half: bi for bi, (half, _, _) in enumerate(blocks)}

        def rows(c):
            half = blocks[c // n_heads][0]
            return slice(half * blk, (half + 1) * blk)

        def cols(c):
            n = c % n_heads
            return slice((n // 2) * LANES, (n // 2 + 1) * LANES)

        def scores(c):
            q = q_ref[0, rows(c), cols(c)] * scale
            k = k_ref[0, pl.ds(starts[c // n_heads], blk), cols(c)]
            qh = jnp.where(head_lanes[c % 2], q, zero_bf)
            return lax.dot_general(qh, k, (((1,), (1,)), ((), ())), preferred_element_type=F32)

        def log_terms(c, z):
            softplus = jnp.maximum(z, 0.0) + jnp.log(1.0 + jnp.exp(jnp.minimum(z, -z)))
            if blocks[c // n_heads][2]:
                softplus = jnp.where(causal, softplus, 0.0)
            return z - softplus, softplus.astype(BF16), softplus[:, 0:1]

        def weighted_values(c, log_beta, rest, first_col):
            half, _, diagonal = blocks[c // n_heads]
            slot = half * n_heads + c % n_heads
            arg = log_beta + rest
            if not diagonal:
                arg = arg + carry_ref[slot]
            w = jnp.exp(arg)
            if diagonal:
                w = jnp.where(causal, w, 0.0)
            v = v_ref[0, pl.ds(starts[c // n_heads], blk), cols(c)]
            vh = jnp.where(head_lanes[c % 2], v, zero_bf)
            block_sum = rest[:, 0:1] - first_col
            carry = block_sum if diagonal else carry_ref[slot] + block_sum
            carry_ref[slot] = carry
            return _dot(w.astype(BF16), vh), carry

        z, terms, rest, pv, bound = {}, {}, {}, {}, {}
        for step in range(n_chains + 3):
            c4, c3, c2, c1 = step - 3, step - 2, step - 1, step
            if 0 <= c3 < n_chains:
                rest[c3] = _dot(terms[c3][1], upper2)
            if 0 <= c1 < n_chains:
                z[c1] = scores(c1)
            if 0 <= c4 < n_chains:
                pv[c4], carry = weighted_values(c4, terms[c4][0], rest.pop(c4), terms[c4][2])
                del terms[c4]
                half, _, diagonal = blocks[c4 // n_heads]
                if last_block_of[half] == c4 // n_heads:
                    bound[half] = carry if half not in bound else jnp.maximum(bound[half], carry)
                if c4 % 2 == 1:
                    both = pv.pop(c4 - 1) + pv.pop(c4)
                    o_ref[0, rows(c4), cols(c4)] = (both if diagonal
                                                    else o_ref[0, rows(c4), cols(c4)] + both)
            if 0 <= c2 < n_chains:
                terms[c2] = log_terms(c2, z.pop(c2))
        return {half: (jnp.max(b) > EXP_ZERO_BELOW).astype(jnp.int32) for half, b in bound.items()}

    def walk_further(half, j0, go):
        def cond(state):
            return jnp.logical_and(state[0] >= 0, state[1] > 0)

        def body(state):
            return state[0] - 1, add_blocks([(half, state[0], False)])[half]

        lax.while_loop(cond, body, (j0, go))

    q0 = Q_BLOCKS_PER_STEP * qi

    @pl.when(qi == 0)
    def _():
        add_blocks([(0, q0, True), (1, q0 + 1, True), (1, q0, False)])

    @pl.when(qi > 0)
    def _():
        go = add_blocks([(0, q0, True), (0, q0 - 1, False), (1, q0 + 1, True), (1, q0, False)])
        walk_further(0, q0 - 2, go[0])
        walk_further(1, q0 - 1, go[1])


def _attn_and_cast_kernel(*refs, n_cast):
    q_ref, k_ref, v_ref = refs[:3]
    srcs = refs[3:3 + n_cast]
    o_ref = refs[3 + n_cast]
    dsts = refs[4 + n_cast:4 + 2 * n_cast]
    carry_ref = refs[4 + 2 * n_cast]
    for src, dst in zip(srcs, dsts):
        dst[...] = src[...].astype(BF16)
    _attn_kernel(q_ref, k_ref, v_ref, o_ref, carry_ref)


BF16_ROWS = 16


def _attention(qkv3, cast_weights=()):
    b, s, _ = qkv3.shape
    blk = min(ATTN_BLOCK, s // Q_BLOCKS_PER_STEP)
    q_rows = Q_BLOCKS_PER_STEP * blk
    assert s % q_rows == 0, (s, q_rows)
    n_q = s // q_rows
    steps = b * n_q
    cast_2d, in_cast, out_cast, out_shapes = [], [], [], []
    for w, index in cast_weights:
        cols = w.shape[-1]
        layer_rows = math.prod(w.shape[1:-1])
        hold = 1
        while steps % hold or layer_rows % (steps // hold) or (layer_rows // (steps // hold)) % BF16_ROWS:
            hold += 1
        n_blocks = steps // hold
        rows = layer_rows // n_blocks

        def in_map(bi, qi, index=index, hold=hold, n_blocks=n_blocks):
            return index * n_blocks + (bi * n_q + qi) // hold, 0

        def out_map(bi, qi, hold=hold):
            return (bi * n_q + qi) // hold, 0

        cast_2d.append(w.reshape(-1, cols))
        in_cast.append(pl.BlockSpec((rows, cols), in_map))
        out_cast.append(pl.BlockSpec((rows, cols), out_map))
        out_shapes.append(jax.ShapeDtypeStruct((layer_rows, cols), BF16))
    outs = pl.pallas_call(
        functools.partial(_attn_and_cast_kernel, n_cast=len(cast_2d)),
        grid=(b, n_q),
        in_specs=[pl.BlockSpec((1, q_rows, ATTN_W), lambda bi, qi: (bi, qi, 0)),
                  pl.BlockSpec((1, s, ATTN_W), lambda bi, qi: (bi, 0, 1)),
                  pl.BlockSpec((1, s, ATTN_W), lambda bi, qi: (bi, 0, 2))] + in_cast,
        out_specs=[pl.BlockSpec((1, q_rows, ATTN_W), lambda bi, qi: (bi, qi, 0))] + out_cast,
        out_shape=[jax.ShapeDtypeStruct((b, s, ATTN_W), F32)] + out_shapes,
        scratch_shapes=[pltpu.VMEM((Q_BLOCKS_PER_STEP * N_HEADS, blk, 1), F32)],
        compiler_params=_cparams(2),
        name="sb_attention",
    )(qkv3, qkv3, qkv3, *cast_2d)
    return outs[0], [o.reshape(w.shape[1:]) for o, (w, _) in zip(outs[1:], cast_weights)]


def _outproj_rows(x_ref, oa_ref, obc_ref, gn_ref, wo_ref):
    oa = _rms(oa_ref[...], gn_ref[:, 0:ATTN_W]).astype(BF16)
    o = jnp.concatenate([oa, obc_ref[...]], axis=1)
    return x_ref[...] + _dot(o, wo_ref[...])


def _swiglu_rows(h_bf16, wg_ref, wu_ref, wd_ref, acc):
    f0 = 0
    for fc in FF_CHUNKS:
        gate = _dot(h_bf16, wg_ref[:, f0:f0 + fc])
        up = _dot(h_bf16, wu_ref[:, f0:f0 + fc])
        act = (gate * _sigmoid(gate) * up).astype(BF16)
        acc = acc + _dot(act, wd_ref[f0:f0 + fc, :])
        f0 += fc
    return acc


def _outproj_dense_kernel(x_ref, oa_ref, obc_ref, gn_ref, wo_ref, g_ref, wg_ref, wu_ref, wd_ref,
                          fg_ref, out_ref, *, final_norm):
    x = _outproj_rows(x_ref, oa_ref, obc_ref, gn_ref, wo_ref)
    h = _rms(x, g_ref[...]).astype(BF16)
    y = _swiglu_rows(h, wg_ref, wu_ref, wd_ref, x)
    if final_norm:
        y = _rms(y, fg_ref[...])
    out_ref[...] = y


def _outproj_dense(x2, oa2, obc, gn_g, wo, g, wg, wu, wd, final_g, final_norm):
    t, d = x2.shape
    f = wg.shape[1]
    assert sum(FF_CHUNKS) == f
    tm = min(TOKEN_TILE, t)
    row = lambda a: a.reshape(1, -1)
    return pl.pallas_call(
        functools.partial(_outproj_dense_kernel, final_norm=final_norm),
        grid=(t // tm,),
        in_specs=[pl.BlockSpec((tm, d), lambda i: (i, 0)),
                  pl.BlockSpec((tm, ATTN_W), lambda i: (i, 0)),
                  pl.BlockSpec((tm, POOL_W + CONV_W), lambda i: (i, 0)),
                  _const_spec((1, d)),
                  _const_spec((d, d)),
                  _const_spec((1, d)),
                  _const_spec((d, f)),
                  _const_spec((d, f)),
                  _const_spec((f, d)),
                  _const_spec((1, d))],
        out_specs=pl.BlockSpec((tm, d), lambda i: (i, 0)),
        out_shape=jax.ShapeDtypeStruct((t, d), F32),
        compiler_params=_cparams(1),
        name="outproj_dense_ffn",
    )(x2, oa2, obc, row(gn_g), wo, row(g), wg, wu, wd, row(final_g))


META_E0, META_E1, META_G0, META_G1, META_R0, META_R1 = range(6)


def _split3(a):
    p0 = a.astype(BF16)
    r1 = a - p0.astype(F32)
    p1 = r1.astype(BF16)
    p2 = (r1 - p1.astype(F32)).astype(BF16)
    return p0, p1, p2


def _outproj_router_kernel(x_ref, oa_ref, obc_ref, gn_ref, wo_ref, g_ref, wrt_ref,
                           x1_ref, meta_ref, counts_ref, run_ref):
    tm = x_ref.shape[0]
    i = pl.program_id(0)
    nt_dims = (((1,), (1,)), ((), ()))

    @pl.when(i == 0)
    def _():
        run_ref[...] = jnp.zeros_like(run_ref)

    x1 = _outproj_rows(x_ref, oa_ref, obc_ref, gn_ref, wo_ref)
    x1_ref[...] = x1
    h = _rms(x1, g_ref[...])
    hs = _split3(h)
    ws = [p.astype(F32) for p in _split3(wrt_ref[...])]
    logits = jnp.zeros((N_EXPERTS, tm), F32)
    for a in (2, 1, 0):
        stacked = jnp.concatenate(ws[:3 - a], axis=0).astype(BF16)
        part = lax.dot_general(stacked, hs[a], nt_dims, preferred_element_type=F32)
        for b in range(3 - a):
            logits = logits + part[b * N_EXPERTS:(b + 1) * N_EXPERTS, :]

    eidx = lax.broadcasted_iota(jnp.int32, (N_EXPERTS, tm), 0).astype(F32)
    neg = jnp.float32(-jnp.inf)
    v0 = jnp.max(logits, axis=0, keepdims=True)
    e0 = jnp.min(jnp.where(logits == v0, eidx, float(N_EXPERTS)), axis=0, keepdims=True)
    masked = jnp.where(eidx == e0, neg, logits)
    v1 = jnp.max(masked, axis=0, keepdims=True)
    e1 = jnp.min(jnp.where(masked == v1, eidx, float(N_EXPERTS)), axis=0, keepdims=True)
    ex = jnp.exp(v1 - v0)
    g0 = 1.0 / (1.0 + ex)
    g1 = ex / (1.0 + ex)

    sel0 = eidx == e0
    sel1 = eidx == e1
    onehot = jnp.where(jnp.logical_or(sel0, sel1), 1.0, 0.0)
    rr = lax.broadcasted_iota(jnp.int32, (tm, tm), 0)
    cc = lax.broadcasted_iota(jnp.int32, (tm, tm), 1)
    earlier = jnp.where(rr < cc, 1.0, 0.0).astype(BF16)
    run = run_ref[:, 0:1]
    rank_all = _dot(onehot.astype(BF16), earlier) + run
    r0 = jnp.sum(jnp.where(sel0, rank_all, 0.0), axis=0, keepdims=True)
    r1 = jnp.sum(jnp.where(sel1, rank_all, 0.0), axis=0, keepdims=True)
    new_run = run + jnp.sum(onehot, axis=1, keepdims=True)
    run_ref[...] = jnp.broadcast_to(new_run, run_ref.shape)
    counts_ref[...] = jnp.broadcast_to(new_run, counts_ref.shape)

    zero_row = jnp.zeros((1, tm), F32)
    meta_ref[...] = jnp.concatenate([e0, e1, g0, g1, r0, r1, zero_row, zero_row], axis=0)


def _outproj_router(x2, oa2, obc, gn_g, wo, g, wr_t):
    t, d = x2.shape
    tm = min(TOKEN_TILE, t)
    row = lambda a: a.reshape(1, -1)
    return pl.pallas_call(
        _outproj_router_kernel,
        grid=(t // tm,),
        in_specs=[pl.BlockSpec((tm, d), lambda i: (i, 0)),
                  pl.BlockSpec((tm, ATTN_W), lambda i: (i, 0)),
                  pl.BlockSpec((tm, POOL_W + CONV_W), lambda i: (i, 0)),
                  _const_spec((1, d)),
                  _const_spec((d, d)),
                  _const_spec((1, d)),
                  _const_spec((N_EXPERTS, d))],
        out_specs=[pl.BlockSpec((tm, d), lambda i: (i, 0)),
                   pl.BlockSpec((SUBLANES, tm), lambda i: (0, i)),
                   pl.BlockSpec((N_EXPERTS, LANES), lambda i: (0, 0))],
        out_shape=[jax.ShapeDtypeStruct((t, d), F32),
                   jax.ShapeDtypeStruct((SUBLANES, t), F32),
                   jax.ShapeDtypeStruct((N_EXPERTS, LANES), F32)],
        scratch_shapes=[pltpu.VMEM((N_EXPERTS, LANES), F32)],
        compiler_params=_cparams(1),
        name="outproj_router",
    )(x2, oa2, obc, row(gn_g), wo, row(g), wr_t)


ROWS_PER_ISSUE = 8


def _row_view(ref, row):
    return ref.at[pl.ds(pl.multiple_of(row * SUBLANES, SUBLANES), SUBLANES), :]


def _dispatch_kernel(pos_ref, pad_start_ref, pad_on_ref, x_ref, g_ref, xs_ref, rows, zeros, sems, zsem):
    tm = x_ref.shape[0]
    i = pl.program_id(0)
    n = pl.num_programs(0)
    slot = lax.rem(i, 2)
    tile_rows = tm * SUBLANES

    def drain(s):
        for _ in range(TOP_K):
            pltpu.make_async_copy(rows.at[s], xs_ref.at[pl.ds(0, tile_rows), :], sems.at[s]).wait()

    @pl.when(i == 0)
    def _():
        zeros[...] = jnp.zeros_like(zeros)
        zrows = zeros.shape[0]
        for e in range(pad_start_ref.shape[0]):
            @pl.when(pad_on_ref[e] > 0)
            def _():
                start = pl.multiple_of(pad_start_ref[e] * SUBLANES, SUBLANES)
                pltpu.make_async_copy(zeros, xs_ref.at[pl.ds(start, zrows), :], zsem).start()
        for e in range(pad_start_ref.shape[0]):
            @pl.when(pad_on_ref[e] > 0)
            def _():
                pltpu.make_async_copy(zeros, xs_ref.at[pl.ds(0, zrows), :], zsem).wait()

    @pl.when(i >= 2)
    def _():
        drain(slot)

    h = _rms(x_ref[...], g_ref[...])
    buf = rows.at[slot]
    for j in range(h.shape[1] // LANES):
        buf[pl.ds(j, tm, stride=SUBLANES), :] = h[:, j * LANES:(j + 1) * LANES]

    def issue(c, _):
        for u in range(ROWS_PER_ISSUE):
            r = c * ROWS_PER_ISSUE + u
            for k in range(TOP_K):
                dst = pos_ref[0, 0, k * tm + r]
                pltpu.make_async_copy(_row_view(buf, r), _row_view(xs_ref, dst),
                                      sems.at[slot]).start(priority=k)
        return 0

    lax.fori_loop(0, tm // ROWS_PER_ISSUE, issue, 0)

    @pl.when(i == n - 1)
    def _():
        @pl.when(n >= 2)
        def _():
            drain(1 - slot)
        drain(slot)


def _tile_positions(pos_t, tm):
    n_tiles = pos_t.shape[1] // tm
    return pos_t.reshape(TOP_K, n_tiles, tm).transpose(1, 0, 2).reshape(n_tiles, 1, TOP_K * tm)


def _dispatch(x2, g, pos_t, pad_start, pad_on, n_rows):
    t, d = x2.shape
    tm = min(TOKEN_TILE, t)
    n_tiles = t // tm
    pos3 = _tile_positions(pos_t, tm)
    grid_spec = pltpu.PrefetchScalarGridSpec(
        num_scalar_prefetch=0,
        grid=(n_tiles,),
        in_specs=[pl.BlockSpec((1, 1, tm * TOP_K), lambda i: (i, 0, 0), memory_space=pltpu.SMEM),
                  pl.BlockSpec(memory_space=pltpu.SMEM),
                  pl.BlockSpec(memory_space=pltpu.SMEM),
                  pl.BlockSpec((tm, d), lambda i: (i, 0)),
                  _const_spec((1, d))],
        out_specs=pl.BlockSpec(memory_space=pl.ANY),
        scratch_shapes=[pltpu.VMEM((2, tm * SUBLANES, LANES), F32),
                        pltpu.VMEM((GROUP_TILE * SUBLANES, LANES), F32),
                        pltpu.SemaphoreType.DMA((2,)),
                        pltpu.SemaphoreType.DMA(())],
    )
    return pl.pallas_call(
        _dispatch_kernel,
        grid_spec=grid_spec,
        out_shape=jax.ShapeDtypeStruct((n_rows * SUBLANES, LANES), F32),
        compiler_params=_cparams(1),
        name="moe_dispatch",
    )(pos3, pad_start, pad_on, x2, g.reshape(1, d))


def _group_ffn_kernel(te_ref, last_ref, xs_ref, wg_ref, wu_ref, wd_ref, ys_ref):
    i = pl.program_id(0)
    tg = xs_ref.shape[0] // SUBLANES
    d = wg_ref.shape[1]

    @pl.when(i <= last_ref[0])
    def _():
        x = jnp.concatenate([xs_ref[pl.ds(j, tg, stride=SUBLANES), :] for j in range(d // LANES)],
                            axis=1).astype(BF16)
        y = _swiglu_rows(x, wg_ref.at[0], wu_ref.at[0], wd_ref.at[0], jnp.zeros((tg, d), F32))
        for j in range(d // LANES):
            ys_ref[pl.ds(j, tg, stride=SUBLANES), :] = y[:, j * LANES:(j + 1) * LANES]

    @pl.when(i > last_ref[0])
    def _():
        ys_ref[...] = jnp.zeros_like(ys_ref)


def _group_ffn(xs, tile_expert, last_tile, wg, wu, wd):
    n_tiles = tile_expert.shape[0]
    _, d, f = wg.shape
    blk = GROUP_TILE * SUBLANES
    grid_spec = pltpu.PrefetchScalarGridSpec(
        num_scalar_prefetch=2,
        grid=(n_tiles,),
        in_specs=[pl.BlockSpec((blk, LANES), lambda i, te, last: (jnp.minimum(i, last[0]), 0)),
                  pl.BlockSpec((1, d, f), lambda i, te, last: (te[i], 0, 0)),
                  pl.BlockSpec((1, d, f), lambda i, te, last: (te[i], 0, 0)),
                  pl.BlockSpec((1, f, d), lambda i, te, last: (te[i], 0, 0))],
        out_specs=pl.BlockSpec((blk, LANES), lambda i, te, last: (i, 0)),
    )
    return pl.pallas_call(
        _group_ffn_kernel,
        grid_spec=grid_spec,
        out_shape=jax.ShapeDtypeStruct(xs.shape, F32),
        compiler_params=_cparams(1),
        name="moe_group_ffn",
    )(tile_expert, last_tile, xs, wg, wu, wd)


def _gather_row(p_ref, ys_ref, bufs, sems, s, tm, r, k):
    src = p_ref[0, 0, k * tm + r]
    pltpu.make_async_copy(_row_view(ys_ref, src), _row_view(bufs.at[s, k], r),
                          sems.at[s]).start(priority=k)


def _gather_tile(p_ref, ys_ref, bufs, sems, s, tm):
    def issue(c, _):
        for u in range(ROWS_PER_ISSUE):
            for k in range(TOP_K):
                _gather_row(p_ref, ys_ref, bufs, sems, s, tm, c * ROWS_PER_ISSUE + u, k)
        return 0
    lax.fori_loop(0, tm // ROWS_PER_ISSUE, issue, 0)


def _wait_tile(ys_ref, bufs, sems, s, tm):
    for k in range(TOP_K):
        pltpu.make_async_copy(ys_ref.at[pl.ds(0, tm * SUBLANES), :], bufs.at[s, k], sems.at[s]).wait()


def _combine_rows(pos_ref, pos_next_ref, x_ref, meta_ref, ys_ref, bufs, sems, spread=0):
    tm = x_ref.shape[0]
    i = pl.program_id(0)
    n = pl.num_programs(0)
    slot = lax.rem(i, 2)

    @pl.when(i == 0)
    def _():
        _gather_tile(pos_ref, ys_ref, bufs, sems, 0, tm)

    if not spread:
        @pl.when(i + 1 < n)
        def _():
            _gather_tile(pos_next_ref, ys_ref, bufs, sems, 1 - slot, tm)

    _wait_tile(ys_ref, bufs, sems, slot, tm)

    batches = iter(range(spread))

    def start_some():
        b = next(batches, None)
        if b is not None:
            per = tm // spread
            for r in range(b * per, (b + 1) * per):
                for k in range(TOP_K):
                    _gather_row(pos_next_ref, ys_ref, bufs, sems, 1 - slot, tm, r, k)

    meta = meta_ref[...]
    g0 = meta[:, META_G0:META_G0 + 1]
    g1 = meta[:, META_G1:META_G1 + 1]
    x = x_ref[...]
    cols = []
    for j in range(x.shape[1] // LANES):
        y0 = bufs[slot, 0, pl.ds(j, tm, stride=SUBLANES), :]
        y1 = bufs[slot, 1, pl.ds(j, tm, stride=SUBLANES), :]
        cols.append(x[:, j * LANES:(j + 1) * LANES] + (g0 * y0 + g1 * y1))
    y = jnp.concatenate(cols, axis=1)
    return (y, start_some) if spread else y


def _combine_kernel(pos_ref, pos_next_ref, x_ref, meta_ref, fg_ref, ys_ref, out_ref, bufs, sems):
    y = _combine_rows(pos_ref, pos_next_ref, x_ref, meta_ref, ys_ref, bufs, sems)
    out_ref[...] = _rms(y, fg_ref[...])


GATHER_SPREAD = 16


def _combine_inproj_mix_kernel(pos_ref, pos_next_ref, x_ref, meta_ref, ys_ref, *refs, tiles_per_seq):
    n_mix_in = 9
    x_out_ref, qkv_ref, obc_ref = refs[n_mix_in:n_mix_in + 3]
    bufs, sems = refs[n_mix_in + 3:n_mix_in + 5]
    mix_scratch = refs[n_mix_in + 5:]
    tm = x_ref.shape[0]
    i = pl.program_id(0)
    x, start_some = _combine_rows(pos_ref, pos_next_ref, x_ref, meta_ref, ys_ref, bufs, sems,
                                  spread=GATHER_SPREAD)
    x_out_ref[...] = x
    _inproj_mix_rows(x, *refs[:n_mix_in], qkv_ref, obc_ref, *mix_scratch, tiles_per_seq=tiles_per_seq,
                     between_stages=start_some)
    for _ in range(GATHER_SPREAD):
        start_some()

    @pl.when(i == pl.num_programs(0) - 1)
    def _():
        _wait_tile(ys_ref, bufs, sems, 1 - lax.rem(i, 2), tm)


def _combine_specs(t, d, tm):
    n_tiles = t // tm
    in_specs = [pl.BlockSpec((1, 1, tm * TOP_K), lambda i: (i, 0, 0), memory_space=pltpu.SMEM),
                pl.BlockSpec((1, 1, tm * TOP_K), lambda i: (jnp.minimum(i + 1, n_tiles - 1), 0, 0),
                             memory_space=pltpu.SMEM),
                pl.BlockSpec((tm, d), lambda i: (i, 0)),
                pl.BlockSpec((tm, SUBLANES), lambda i: (i, 0))]
    scratch = [pltpu.VMEM((2, TOP_K, tm * SUBLANES, LANES), F32), pltpu.SemaphoreType.DMA((2,))]
    return in_specs, scratch


def _combine(x2, meta, pos_t, ys, final_g):
    t, d = x2.shape
    tm = min(GROUP_TILE, t)
    pos3 = _tile_positions(pos_t, tm)
    in_specs, scratch = _combine_specs(t, d, tm)
    return pl.pallas_call(
        _combine_kernel,
        grid=(t // tm,),
        in_specs=in_specs + [_const_spec((1, d)), pl.BlockSpec(memory_space=pl.ANY)],
        out_specs=pl.BlockSpec((tm, d), lambda i: (i, 0)),
        out_shape=jax.ShapeDtypeStruct((t, d), F32),
        scratch_shapes=scratch,
        compiler_params=_cparams(1),
        name="moe_combine",
    )(pos3, pos3, x2, meta, final_g.reshape(1, d), ys)


def _combine_inproj_mix(x2, meta, pos_t, ys, seq_len, g, w_bf16, wp_bd, pool_scale, conv_w, conv_b,
                        ln_g, ln_b, gn_g):
    t, d = x2.shape
    tm = min(TOKEN_TILE, seq_len)
    pos3 = _tile_positions(pos_t, tm)
    in_specs, scratch = _combine_specs(t, d, tm)
    mix_in, mix_out, mix_shapes, mix_scratch = _inproj_mix_specs(t, d, w_bf16.shape[1], tm)
    row = lambda a: a.reshape(1, -1)
    return pl.pallas_call(
        functools.partial(_combine_inproj_mix_kernel, tiles_per_seq=seq_len // tm),
        grid=(t // tm,),
        in_specs=in_specs + [pl.BlockSpec(memory_space=pl.ANY)] + mix_in,
        out_specs=[pl.BlockSpec((tm, d), lambda i: (i, 0))] + mix_out,
        out_shape=[jax.ShapeDtypeStruct((t, d), F32)] + mix_shapes,
        scratch_shapes=scratch + mix_scratch,
        compiler_params=_cparams(1),
        name="combine_inproj_mix",
    )(pos3, pos3, x2, meta, ys, row(g), w_bf16, wp_bd, row(pool_scale), conv_w, row(conv_b),
      row(ln_g), row(ln_b), row(gn_g))


def _moe_expert_rows(x2, meta_t, counts, g, wg, wu, wd):
    t, d = x2.shape
    meta = meta_t.T

    cnt = counts[:, 0].astype(jnp.int32)
    padded = ((cnt + GROUP_TILE - 1) // GROUP_TILE) * GROUP_TILE
    ends = jnp.cumsum(padded)
    offs = ends - padded
    n_tiles = (t * TOP_K) // GROUP_TILE + N_EXPERTS
    n_rows = n_tiles * GROUP_TILE
    eid_t = meta_t[META_E0:META_E1 + 1].astype(jnp.int32)
    rank_t = meta_t[META_R0:META_R1 + 1].astype(jnp.int32)
    pos_t = rank_t
    for e in range(N_EXPERTS):
        pos_t = pos_t + jnp.where(eid_t == e, offs[e], 0)
    tail_tiles = ends[-1] + jnp.arange(N_EXPERTS, dtype=jnp.int32) * GROUP_TILE
    pad_start = jnp.concatenate([ends - GROUP_TILE, tail_tiles])
    pad_on = jnp.concatenate([cnt > 0, tail_tiles < n_rows]).astype(jnp.int32)
    pad_start = jnp.clip(pad_start, 0, n_rows - GROUP_TILE).astype(jnp.int32)
    tile_start = jnp.arange(n_tiles, dtype=jnp.int32) * GROUP_TILE
    last_tile = jnp.maximum(ends[-1] // GROUP_TILE - 1, 0).astype(jnp.int32).reshape(1)
    tile_expert = jnp.sum(tile_start[:, None] >= ends[None, :], axis=1).astype(jnp.int32)
    tile_expert = jnp.minimum(tile_expert, tile_expert[last_tile[0]])

    xs = _dispatch(x2, g, pos_t, pad_start, pad_on, n_rows)
    ys = _group_ffn(xs, tile_expert, last_tile, wg, wu, wd)
    return meta, pos_t, ys


def kernel(x, attn_norm_g, w_in, pool_w, pool_scale, conv_w, conv_b, conv_ln_g, conv_ln_b,
           group_norm_g, w_out, ffn_norm_g, dense_w_gate, dense_w_up, dense_w_down,
           router_w, moe_w_gate, moe_w_up, moe_w_down, final_norm_g):
    b, s, d = x.shape
    depth = w_in.shape[0]
    t = b * s
    def mix_params(l, w_in_bf16):
        wp_bd = jax.scipy.linalg.block_diag(*[pool_w[l, gi] for gi in range(pool_w.shape[1])]).astype(BF16)
        return (attn_norm_g[l], w_in_bf16, wp_bd, pool_scale[l], conv_w[l], conv_b[l], conv_ln_g[l],
                conv_ln_b[l], group_norm_g[l])

    x2 = x.reshape(t, d)
    qkv, obc = _inproj_mix(x2, s, *mix_params(0, w_in[0].astype(BF16)))
    for l in range(depth):
        last = l == depth - 1
        i = l // 2
        ffn_f32 = ((dense_w_gate, dense_w_up, dense_w_down) if l % 2 == 0
                   else (moe_w_gate, moe_w_up, moe_w_down))
        jobs = [(w, i) for w in ffn_f32] + [(w_out, l)] + ([] if last else [(w_in, l + 1)])
        o_a, casts = _attention(qkv.reshape(b, s, -1), jobs)
        ffn_bf16, w_out_bf16 = casts[:3], casts[3]
        next_mix = None if last else mix_params(l + 1, casts[4])
        oa2 = o_a.reshape(t, -1)
        if l % 2 == 0:
            x2 = _outproj_dense(x2, oa2, obc, group_norm_g[l], w_out_bf16, ffn_norm_g[l], *ffn_bf16,
                                final_norm_g, last)
            if not last:
                qkv, obc = _inproj_mix(x2, s, *next_mix)
        else:
            x1, meta_t, counts = _outproj_router(x2, oa2, obc, group_norm_g[l], w_out_bf16,
                                                 ffn_norm_g[l], router_w[i].T)
            meta, pos_t, ys = _moe_expert_rows(x1, meta_t, counts, ffn_norm_g[l], *ffn_bf16)
            if last:
                x2 = _combine(x1, meta, pos_t, ys, final_norm_g)
            else:
                x2, qkv, obc = _combine_inproj_mix(x1, meta, pos_t, ys, s, *next_mix)
    return x2.reshape(b, s, d)
```
